```python
import math
import jax, jax.numpy as jnp
from jax import lax
import numpy as np

D_MODEL = 1024
BATCH = 8
SEQ = 2048
DEPTH = 2
DEC_BATCH = 32
DEC_SEQ = 8
PAST_LEN = 16384
PAGE_SIZE = 128

HEAD_DIM = 64
LRU_WIDTH = D_MODEL // 4
LRU_BLOCKS = LRU_WIDTH // HEAD_DIM
CONV_W = 4
LRU_C = 8.0
M_HEADS = (D_MODEL // 4) // HEAD_DIM
M_WIDTH = M_HEADS * HEAD_DIM
FOX_HEADS = (D_MODEL // 2) // HEAD_DIM
FOX_WIDTH = FOX_HEADS * HEAD_DIM
MIX_WIDTH = LRU_WIDTH + M_WIDTH + FOX_WIDTH
CHUNK = 128
Q_BLOCK = 128
N_GROUPS = 4
EXPERTS_PER_GROUP = 4
N_EXPERTS = N_GROUPS * EXPERTS_PER_GROUP
TOP_K = 2
EXPERT_FF = D_MODEL // 4
EPS = 1e-6
PROJ_SPLITS = (LRU_WIDTH, LRU_WIDTH,
               M_WIDTH, M_WIDTH, M_WIDTH, M_WIDTH,
               M_HEADS, M_HEADS,
               FOX_WIDTH, FOX_WIDTH, FOX_WIDTH,
               FOX_HEADS)
PROJ_WIDTH = 2 * LRU_WIDTH + 4 * M_WIDTH + 2 * M_HEADS + 3 * FOX_WIDTH + FOX_HEADS

kernel_name = 'hybrid_rglru_mlstm_fox_hmoe_step'


def _rmsnorm(x, g):
    xf = x.astype(jnp.float32)
    y = xf * lax.rsqrt(jnp.mean(xf * xf, axis=-1, keepdims=True) + EPS)
    return y.astype(x.dtype) * g


def _split_proj(z):
    parts, off = [], 0
    for width in PROJ_SPLITS:
        parts.append(z[..., off:off + width])
        off += width
    return parts


def _causal_conv(u, buf, w, b):
    T = u.shape[1]
    up = jnp.concatenate([buf.astype(u.dtype), u], axis=1)
    y = up[:, 0:T] * w[0]
    for j in range(1, CONV_W):
        y = y + up[:, j:j + T] * w[j]
    return y + b, up[:, T:]


def _block_diag(u, w, b):
    B, T, _ = u.shape
    ub = u.reshape(B, T, LRU_BLOCKS, LRU_WIDTH // LRU_BLOCKS)
    return jnp.einsum('btnc,ncd->btnd', ub, w).reshape(B, T, LRU_WIDTH) + b


def _rg_lru(u, h0, wr, br, wi, bi, lam):
    uf = u.astype(jnp.float32)
    r = jax.nn.sigmoid(_block_diag(uf, wr, br))
    i = jax.nn.sigmoid(_block_diag(uf, wi, bi))
    log_a = -LRU_C * r * jax.nn.softplus(-lam.astype(jnp.float32))
    a = jnp.exp(log_a)
    bx = jnp.sqrt(-jnp.expm1(2.0 * log_a)) * (i * uf)
    bx = bx.at[:, 0].add(a[:, 0] * h0.astype(jnp.float32))

    def combine(left, right):
        a_l, b_l = left
        a_r, b_r = right
        return a_l * a_r, a_r * b_l + b_r

    _, h = lax.associative_scan(combine, (a, bx), axis=1)
    return h, h[:, -1]


def _mlstm(q, k, v, ig, logf, C0, n0, m0):
    f32 = jnp.float32
    B, T, H, Dh = q.shape
    L = CHUNK if T % CHUNK == 0 else T
    nc = T // L
    q = q.astype(f32)
    k = k.astype(f32) * (Dh ** -0.5)
    v = v.astype(f32)

    def chunks(a):
        return jnp.moveaxis(a.reshape((B, nc, L) + a.shape[2:]), 1, 0)

    causal = jnp.tril(jnp.ones((L, L), dtype=bool))[None, :, :, None]

    def step(carry, inp):
        C, n, m = carry
        qc, kc, vc, ic, fc = inp
        bc = jnp.cumsum(fc, axis=1)
        dlog = bc[:, :, None, :] - bc[:, None, :, :] + ic[:, None, :, :]
        dlog = jnp.where(causal, dlog, -jnp.inf)
        inter = bc + m[:, None, :]
        mt = jnp.maximum(inter, jnp.max(dlog, axis=2))
        sqk = jnp.einsum('bthd,bshd->btsh', qc, kc) * jnp.exp(dlog - mt[:, :, None, :])
        decay = jnp.exp(inter - mt)
        num = (jnp.einsum('btsh,bshd->bthd', sqk, vc)
               + decay[..., None] * jnp.einsum('bthk,bhkv->bthv', qc, C))
        den = jnp.sum(sqk, axis=2) + decay * jnp.einsum('bthk,bhk->bth', qc, n)
        h = num / jnp.maximum(jnp.abs(den), jnp.exp(-mt))[..., None]
        m_new = mt[:, -1]
        wk = jnp.exp(bc[:, -1:] - bc + ic - m_new[:, None, :])
        g = jnp.exp(bc[:, -1] + m - m_new)
        C_new = g[..., None, None] * C + jnp.einsum('bsh,bshk,bshv->bhkv', wk, kc, vc)
        n_new = g[..., None] * n + jnp.einsum('bsh,bshk->bhk', wk, kc)
        return (C_new, n_new, m_new), h

    init = (C0.astype(f32), n0.astype(f32), m0.astype(f32))
    (C, n, m), h = lax.scan(step, init, (chunks(q), chunks(k), chunks(v),
                                          chunks(ig.astype(f32)), chunks(logf.astype(f32))))
    return jnp.moveaxis(h, 0, 1).reshape(B, T, H, Dh), C, n, m


def _fox_prompt(q, k, v, logf):
    B, S, H, Dh = q.shape
    nb = S // Q_BLOCK
    F = jnp.cumsum(logf.astype(jnp.float32), axis=1).transpose(0, 2, 1)
    qb = jnp.moveaxis(q.reshape(B, nb, Q_BLOCK, H, Dh), 1, 0)
    Fb = jnp.moveaxis(F.reshape(B, H, nb, Q_BLOCK), 2, 0)
    kpos = jnp.arange(S)
    scale = HEAD_DIM ** -0.5

    def one_block(args):
        qi, Fi, bi = args
        s = jnp.einsum('bqhd,bkhd->bhqk', qi, k).astype(jnp.float32) * scale
        s = s + Fi[..., None] - F[:, :, None, :]
        qpos = bi * Q_BLOCK + jnp.arange(Q_BLOCK)
        s = jnp.where(kpos[None, :] <= qpos[:, None], s, -jnp.inf)
        p = jax.nn.softmax(s, axis=-1)
        return jnp.einsum('bhqk,bkhd->bqhd', p.astype(v.dtype), v)

    o = lax.map(one_block, (qb, Fb, jnp.arange(nb)))
    return jnp.moveaxis(o, 0, 1).reshape(B, S, H, Dh)


def _fox_decode(q, k, v, logf, k_past, v_past, logf_past):
    f32 = jnp.float32
    T = q.shape[1]
    P = k_past.shape[1]
    scale = HEAD_DIM ** -0.5
    rev = lax.cumsum(logf_past.astype(f32), axis=1, reverse=True)
    R = jnp.concatenate([rev[:, 1:], jnp.zeros_like(rev[:, :1])], axis=1).transpose(0, 2, 1)
    G = jnp.cumsum(logf.astype(f32), axis=1).transpose(0, 2, 1)
    s_past = (jnp.einsum('bthd,bshd->bhts', q, k_past).astype(f32) * scale
              + G[..., None] + R[:, :, None, :])
    s_new = (jnp.einsum('bthd,bshd->bhts', q, k).astype(f32) * scale
             + G[..., None] - G[:, :, None, :])
    causal = jnp.tril(jnp.ones((T, T), dtype=bool))
    s_new = jnp.where(causal, s_new, -jnp.inf)
    p = jax.nn.softmax(jnp.concatenate([s_past, s_new], axis=-1), axis=-1).astype(v.dtype)
    return (jnp.einsum('bhts,bshd->bthd', p[..., :P], v_past)
            + jnp.einsum('bhts,bshd->bthd', p[..., P:], v))


def _hier_moe(xn, wr_g, br_g, wr_e, br_e, w_gate, w_up, w_down):
    f32 = jnp.float32
    shape = xn.shape
    t = xn.reshape(-1, shape[-1])
    lg = (t @ wr_g + br_g).astype(f32)
    pg = jax.nn.softmax(lg, axis=-1)
    g_sel = jnp.argmax(lg, axis=-1)
    g_oh = jax.nn.one_hot(g_sel, N_GROUPS, dtype=f32)
    le = (t @ wr_e + br_e).astype(f32).reshape(-1, N_GROUPS, EXPERTS_PER_GROUP)
    le_sel = jnp.sum(le * g_oh[:, :, None], axis=1)
    top_v, top_i = lax.top_k(le_sel, TOP_K)
    w = jax.nn.softmax(top_v, axis=-1) * jnp.sum(pg * g_oh, axis=-1, keepdims=True)
    e_id = g_sel[:, None] * EXPERTS_PER_GROUP + top_i
    gates = jnp.sum(jax.nn.one_hot(e_id, N_EXPERTS, dtype=f32) * w[..., None], axis=1).astype(t.dtype)
    out = jnp.zeros_like(t)
    for e in range(N_EXPERTS):
        h = jax.nn.silu(t @ w_gate[e]) * (t @ w_up[e])
        out = out + gates[:, e:e + 1] * (h @ w_down[e])
    return out.reshape(shape)


def setup_inputs(seed: int = 0) -> dict:
    key = jax.random.key(seed)
    ks = jax.random.split(key, 40)
    f32 = jnp.float32
    n_pages = PAST_LEN // PAGE_SIZE
    n_used = DEC_BATCH * n_pages
    n_pool = n_used + max(1, n_used // 4)

    def nrm(k, shp, s):
        return jax.random.normal(k, shp, f32) * s

    bs = LRU_WIDTH // LRU_BLOCKS
    a0 = jax.random.uniform(ks[18], (DEPTH, LRU_WIDTH), f32, 0.9, 0.999)
    s0 = a0 ** (1.0 / LRU_C)
    return {
        'x_prompt': nrm(ks[0], (BATCH, SEQ, D_MODEL), 1.0),
        'x_sample': nrm(ks[1], (DEC_BATCH, DEC_SEQ, D_MODEL), 1.0),
        'cache_k': nrm(ks[2], (DEPTH, n_pool, PAGE_SIZE, FOX_HEADS, HEAD_DIM), 1.0),
        'cache_v': nrm(ks[3], (DEPTH, n_pool, PAGE_SIZE, FOX_HEADS, HEAD_DIM), 1.0),
        'cache_logf': jax.nn.log_sigmoid(jax.random.uniform(ks[4], (DEPTH, n_pool, PAGE_SIZE, FOX_HEADS), f32, 1.0, 6.0)),
        'page_table': jax.random.permutation(ks[5], n_pool)[:n_used].reshape(DEC_BATCH, n_pages).astype(jnp.int32),
        'state_conv': nrm(ks[6], (DEPTH, DEC_BATCH, CONV_W - 1, LRU_WIDTH), 1.0),
        'state_lru_h': nrm(ks[7], (DEPTH, DEC_BATCH, LRU_WIDTH), 0.5),
        'state_mlstm_C': nrm(ks[8], (DEPTH, DEC_BATCH, M_HEADS, HEAD_DIM, HEAD_DIM), 0.3),
        'state_mlstm_n': nrm(ks[9], (DEPTH, DEC_BATCH, M_HEADS, HEAD_DIM), 0.3),
        'state_mlstm_m': nrm(ks[10], (DEPTH, DEC_BATCH, M_HEADS), 1.0),
        'norm1': 1.0 + nrm(ks[11], (DEPTH, D_MODEL), 0.01),
        'w_in': nrm(ks[12], (DEPTH, D_MODEL, PROJ_WIDTH), D_MODEL ** -0.5),
        'conv_w': nrm(ks[13], (DEPTH, CONV_W, LRU_WIDTH), CONV_W ** -0.5),
        'conv_b': nrm(ks[14], (DEPTH, LRU_WIDTH), 0.02),
        'lru_wr': nrm(ks[15], (DEPTH, LRU_BLOCKS, bs, bs), bs ** -0.5),
        'lru_br': nrm(ks[16], (DEPTH, LRU_WIDTH), 0.02),
        'lru_wi': nrm(ks[17], (DEPTH, LRU_BLOCKS, bs, bs), bs ** -0.5),
        'lru_bi': nrm(ks[19], (DEPTH, LRU_WIDTH), 0.02),
        'lru_lambda': jnp.log(s0) - jnp.log1p(-s0),
        'm_bi': nrm(ks[20], (DEPTH, M_HEADS), 0.1),
        'm_bf': jax.random.uniform(ks[21], (DEPTH, M_HEADS), f32, 3.0, 6.0),
        'fox_bf': jax.random.uniform(ks[22], (DEPTH, FOX_HEADS), f32, 1.0, 5.0),
        'out_norm': 1.0 + nrm(ks[23], (DEPTH, MIX_WIDTH), 0.01),
        'w_out': nrm(ks[24], (DEPTH, MIX_WIDTH, D_MODEL), MIX_WIDTH ** -0.5),
        'norm2': 1.0 + nrm(ks[25], (DEPTH, D_MODEL), 0.01),
        'router_g': nrm(ks[26], (DEPTH, D_MODEL, N_GROUPS), D_MODEL ** -0.5),
        'router_g_b': nrm(ks[27], (DEPTH, N_GROUPS), 0.01),
        'router_e': nrm(ks[28], (DEPTH, D_MODEL, N_EXPERTS), D_MODEL ** -0.5),
        'router_e_b': nrm(ks[29], (DEPTH, N_EXPERTS), 0.01),
        'w_gate': nrm(ks[30], (DEPTH, N_EXPERTS, D_MODEL, EXPERT_FF), D_MODEL ** -0.5),
        'w_up': nrm(ks[31], (DEPTH, N_EXPERTS, D_MODEL, EXPERT_FF), D_MODEL ** -0.5),
        'w_down': nrm(ks[32], (DEPTH, N_EXPERTS, EXPERT_FF, D_MODEL), EXPERT_FF ** -0.5),
        'final_norm': 1.0 + nrm(ks[33], (D_MODEL,), 0.01),
    }


def reference(x_prompt, x_sample, cache_k, cache_v, cache_logf, page_table,
              state_conv, state_lru_h, state_mlstm_C, state_mlstm_n, state_mlstm_m,
              norm1, w_in, conv_w, conv_b, lru_wr, lru_br, lru_wi, lru_bi, lru_lambda,
              m_bi, m_bf, fox_bf, out_norm, w_out, norm2,
              router_g, router_g_b, router_e, router_e_b, w_gate, w_up, w_down, final_norm):
    f32 = jnp.float32

    def layer(x, l, conv_buf, h0, C0, n0, m0, fox_attend):
        B, T, _ = x.shape
        xn = _rmsnorm(x, norm1[l])
        z = jnp.einsum('btd,dp->btp', xn, w_in[l])
        xl, gl, mq, mk, mv, mo, mi, mf, fq, fk, fv, ff = _split_proj(z)

        def heads(a, n):
            return a.reshape(B, T, n, HEAD_DIM)

        xc, conv_new = _causal_conv(xl, conv_buf, conv_w[l], conv_b[l])
        hl, h_new = _rg_lru(xc, h0, lru_wr[l], lru_br[l], lru_wi[l], lru_bi[l], lru_lambda[l])
        y_lru = hl.astype(x.dtype) * jax.nn.gelu(gl, approximate=True)

        ig = mi.astype(f32) + m_bi[l]
        lf_m = jax.nn.log_sigmoid(mf.astype(f32) + m_bf[l])
        hm, C_new, n_new, m_new = _mlstm(heads(mq, M_HEADS), heads(mk, M_HEADS), heads(mv, M_HEADS),
                                         ig, lf_m, C0, n0, m0)
        y_m = jax.nn.sigmoid(mo) * hm.reshape(B, T, M_WIDTH).astype(x.dtype)

        lf_f = jax.nn.log_sigmoid(ff.astype(f32) + fox_bf[l])
        kf = heads(fk, FOX_HEADS)
        vf = heads(fv, FOX_HEADS)
        y_f = fox_attend(l, heads(fq, FOX_HEADS), kf, vf, lf_f).reshape(B, T, FOX_WIDTH)

        g = out_norm[l]
        y = jnp.concatenate([_rmsnorm(y_lru, g[:LRU_WIDTH]),
                             _rmsnorm(y_m, g[LRU_WIDTH:LRU_WIDTH + M_WIDTH]),
                             _rmsnorm(y_f, g[LRU_WIDTH + M_WIDTH:])], axis=-1)
        x = x + jnp.einsum('btm,md->btd', y, w_out[l])
        x = x + _hier_moe(_rmsnorm(x, norm2[l]), router_g[l], router_g_b[l], router_e[l],
                          router_e_b[l], w_gate[l], w_up[l], w_down[l])
        return x, (kf, vf, lf_f, conv_new, h_new, C_new, n_new, m_new)

    def fox_prompt(l, q, k, v, lf):
        return _fox_prompt(q, k, v, lf)

    def fox_sample(l, q, k, v, lf):
        db = page_table.shape[0]
        k_past = cache_k[l, page_table].reshape(db, -1, FOX_HEADS, HEAD_DIM)
        v_past = cache_v[l, page_table].reshape(db, -1, FOX_HEADS, HEAD_DIM)
        lf_past = cache_logf[l, page_table].reshape(db, -1, FOX_HEADS)
        return _fox_decode(q, k, v, lf, k_past, v_past, lf_past)

    bp = x_prompt.shape[0]
    zc = jnp.zeros((bp, CONV_W - 1, LRU_WIDTH), x_prompt.dtype)
    zh = jnp.zeros((bp, LRU_WIDTH), f32)
    zC = jnp.zeros((bp, M_HEADS, HEAD_DIM, HEAD_DIM), f32)
    zn = jnp.zeros((bp, M_HEADS, HEAD_DIM), f32)
    zm = jnp.zeros((bp, M_HEADS), f32)

    xp, xs = x_prompt, x_sample
    st_p, st_s = [], []
    for l in range(DEPTH):
        xp, sp = layer(xp, l, zc, zh, zC, zn, zm, fox_prompt)
        xs, ss = layer(xs, l, state_conv[l], state_lru_h[l], state_mlstm_C[l],
                       state_mlstm_n[l], state_mlstm_m[l], fox_sample)
        st_p.append(sp)
        st_s.append(ss)

    def stk(states, i):
        return jnp.stack([s[i] for s in states])

    y_prompt = _rmsnorm(xp, final_norm)
    y_sample = _rmsnorm(xs, final_norm)
    k_p, v_p, logf_p, conv_p = stk(st_p, 0), stk(st_p, 1), stk(st_p, 2), stk(st_p, 3)
    h_p, C_p, n_p, m_p = stk(st_p, 4), stk(st_p, 5), stk(st_p, 6), stk(st_p, 7)
    k_s, v_s, logf_s, conv_s = stk(st_s, 0), stk(st_s, 1), stk(st_s, 2), stk(st_s, 3)
    h_s, C_s, n_s, m_s = stk(st_s, 4), stk(st_s, 5), stk(st_s, 6), stk(st_s, 7)
    return (y_prompt, y_sample, k_p, v_p, logf_p, conv_p, h_p, C_p, n_p, m_p,
            k_s, v_s, logf_s, conv_s, h_s, C_s, n_s, m_s)
```

```python
import functools
import math

import jax
import jax.numpy as jnp
from jax import lax
from jax.experimental import pallas as pl
from jax.experimental.pallas import tpu as pltpu

F32 = jnp.float32
BF16 = jnp.bfloat16

HEAD_DIM = 64
LRU_C = 8.0
CONV_W = 4
CHUNK = 128
N_GROUPS = 4
EXPERTS_PER_GROUP = 4
EPS = 1e-6
NEG = -1e30
LANES = 128
SUBLANES = 8
GATE_W = LANES

_HI = lax.Precision.HIGHEST


def _dot(a, b):
    return jnp.dot(a, b, preferred_element_type=F32)


def _dot_hi(a, b):
    return jnp.dot(a, b, preferred_element_type=F32, precision=_HI)


def _dot_nt(a, b):
    return lax.dot_general(a, b, (((1,), (1,)), ((), ())), preferred_element_type=F32)


def _split(a):
    hi = a.astype(BF16)
    return hi, (a - hi.astype(F32)).astype(BF16)


def _mm(a, b, hp, nt=False):
    dot = _dot_nt if nt else _dot
    if not hp:
        return dot(a.astype(BF16), b.astype(BF16))
    ah, al = _split(a)
    bh, bl = _split(b)
    return dot(ah, bh) + (dot(al, bh) + dot(ah, bl))


def _log_sigmoid(x):
    return jnp.minimum(x, 0.0) - jnp.log1p(jnp.exp(-jnp.abs(x)))


def _softplus(x):
    return jnp.maximum(x, 0.0) + jnp.log1p(jnp.exp(-jnp.abs(x)))


def _gelu_tanh(x):
    return 0.5 * x * (1.0 + jnp.tanh(math.sqrt(2.0 / math.pi) * (x + 0.044715 * (x * x * x))))


def _rms(x):
    return x * lax.rsqrt(jnp.mean(x * x, axis=-1, keepdims=True) + EPS)


def _in_proj_kernel(x_ref, g_ref, w_ref, wkv_ref, lru_ref, m_ref, fqb_ref, gate_ref, *kv_refs, kv_t, hp):
    xn = _rms(x_ref[...]) * g_ref[...]
    if not hp:
        xn = xn.astype(BF16)
    off = 0
    for ref in (lru_ref, m_ref, fqb_ref, gate_ref):
        width = ref.shape[-1]
        r = _mm(xn, w_ref[:, off:off + width], hp)
        off += width
        if ref is fqb_ref:
            r = r * (HEAD_DIM ** -0.5)
        ref[...] = r.astype(ref.dtype)
    if kv_t:
        kt_ref, vt_ref, ktb_ref, vtb_ref = kv_refs
        fw = kt_ref.shape[0]
        kv = _mm(wkv_ref[...], xn, hp, nt=True)
        kt_ref[...] = kv[0:fw, :]
        vt_ref[...] = kv[fw:2 * fw, :]
        ktb_ref[...] = kv[0:fw, :].astype(BF16)
        vtb_ref[...] = kv[fw:2 * fw, :].astype(BF16)
    else:
        k_ref, v_ref = kv_refs
        fw = k_ref.shape[-1]
        kv = _mm(xn, wkv_ref[...], hp)
        k_ref[...] = kv[:, 0:fw]
        v_ref[...] = kv[:, fw:2 * fw]


def _in_proj(x, g, w, wkv, dims, tm, kv_t, seq, hp):
    n, d = x.shape
    lw, mw, fw = dims
    widths = (2 * lw, 4 * mw, fw, GATE_W)
    dtypes = (F32, F32, F32 if hp else BF16, F32)
    row = lambda i: (i, 0)
    fixed = lambda i: (0, 0)
    out_specs = [pl.BlockSpec((tm, wd), row) for wd in widths]
    out_shape = [jax.ShapeDtypeStruct((n, wd), dt) for wd, dt in zip(widths, dtypes)]
    if kv_t:
        per_b = seq // tm
        tmap = lambda i: (i // per_b, 0, i % per_b)
        out_specs += [pl.BlockSpec((None, fw, tm), tmap)] * 4
        out_shape += [jax.ShapeDtypeStruct((n // seq, fw, seq), dt) for dt in (F32, F32, BF16, BF16)]
    else:
        out_specs += [pl.BlockSpec((tm, fw), row)] * 2
        out_shape += [jax.ShapeDtypeStruct((n, fw), F32)] * 2
    return pl.pallas_call(
        functools.partial(_in_proj_kernel, kv_t=kv_t, hp=hp),
        grid=(n // tm,),
        in_specs=[pl.BlockSpec((tm, d), row), pl.BlockSpec((1, d), fixed), pl.BlockSpec(w.shape, fixed),
                  pl.BlockSpec(wkv.shape, fixed)],
        out_specs=out_specs,
        out_shape=out_shape,
        compiler_params=pltpu.CompilerParams(dimension_semantics=("arbitrary",)),
        name="in_proj",
    )(x, g.reshape(1, d), w, wkv)


def _gates_kernel(z_ref, b_ref, act_ref, cum_ref, actT_ref, cumT_ref, carry_ref, *, n_in, n_local):
    c = pl.program_id(1)
    L = z_ref.shape[0]

    @pl.when(c == 0)
    def _():
        carry_ref[...] = jnp.zeros_like(carry_ref)

    pre = z_ref[...] + b_ref[...]
    lane = lax.broadcasted_iota(jnp.int32, pre.shape, 1)
    act = jnp.where(lane < n_in, pre, _log_sigmoid(pre))
    r = lax.broadcasted_iota(jnp.int32, (L, L), 0)
    s = lax.broadcasted_iota(jnp.int32, (L, L), 1)
    tri = jnp.where(s <= r, 1.0, 0.0).astype(F32)
    local = _dot_hi(tri, act)
    carry = carry_ref[0:1, :]
    cum = local + jnp.where(lane[0:1, :] >= n_local, carry, 0.0)
    carry_ref[...] = jnp.broadcast_to(cum[L - 1:L, :], carry_ref.shape)
    act_ref[...] = act
    cum_ref[...] = cum
    rows = actT_ref.shape[0]
    actT_ref[...] = act.T[0:rows, :]
    cumT_ref[...] = cum.T[0:rows, :]


def _gates(z, bias, n_in, n_local, rows):
    b, t, gw = z.shape
    L = CHUNK
    blk = pl.BlockSpec((None, L, gw), lambda i, c: (i, c, 0))
    blk_t = pl.BlockSpec((None, rows, L), lambda i, c: (i, 0, c))
    return pl.pallas_call(
        functools.partial(_gates_kernel, n_in=n_in, n_local=n_local),
        grid=(b, t // L),
        in_specs=[blk, pl.BlockSpec((1, gw), lambda i, c: (0, 0))],
        out_specs=[blk, blk, blk_t, blk_t],
        out_shape=[jax.ShapeDtypeStruct((b, t, gw), F32)] * 2 + [jax.ShapeDtypeStruct((b, rows, t), F32)] * 2,
        scratch_shapes=[pltpu.VMEM((SUBLANES, gw), F32)],
        compiler_params=pltpu.CompilerParams(dimension_semantics=("arbitrary", "arbitrary")),
        name="gates",
    )(z, bias)


def _lru_kernel(u_ref, cbuf_ref, h0_ref, cw_ref, cb_ref, wg_ref, bg_ref, lam_ref,
                y_ref, cnew_ref, hnew_ref, ext_ref, hc_ref, *, hp):
    t = pl.program_id(1)
    nt = pl.num_programs(1)
    tt = y_ref.shape[0]
    c = y_ref.shape[1]
    pad = SUBLANES

    @pl.when(t == 0)
    def _():
        ext_ref[0:pad, :] = cbuf_ref[...]
        hc_ref[...] = jnp.broadcast_to(h0_ref[...], hc_ref.shape)

    u = u_ref[:, 0:c]
    gl = u_ref[:, c:2 * c]
    ext_ref[pad:pad + tt, :] = u
    ext = ext_ref[...]
    xc = cw_ref[CONV_W - 1:CONV_W, :] * u + cb_ref[...]
    for j in range(CONV_W - 1):
        xc = xc + cw_ref[j:j + 1, :] * pltpu.roll(ext, CONV_W - 1 - j, 0)[pad:pad + tt, :]
    ext_ref[0:pad, :] = ext_ref[tt:tt + pad, :]

    pre = _mm(xc, wg_ref[...], hp) + bg_ref[...]
    r = jax.nn.sigmoid(pre[:, 0:c])
    i = jax.nn.sigmoid(pre[:, c:2 * c])
    log_a = (-LRU_C) * r * _softplus(-lam_ref[...])
    a = jnp.exp(log_a)
    th = jnp.tanh(log_a)
    bx = jnp.sqrt(-2.0 * th / (1.0 - th)) * (i * xc)

    row = lax.broadcasted_iota(jnp.int32, (tt, c), 0) & (SUBLANES - 1)
    s = 1
    while s < SUBLANES:
        keep = row >= s
        bx = jnp.where(keep, a * pltpu.roll(bx, s, 0) + bx, bx)
        a = jnp.where(keep, a * pltpu.roll(a, s, 0), a)
        s *= 2
    carry = hc_ref[...]
    gate = _gelu_tanh(gl)
    for g in range(tt // SUBLANES):
        lo = g * SUBLANES
        h = a[lo:lo + SUBLANES, :] * carry + bx[lo:lo + SUBLANES, :]
        y_ref[lo:lo + SUBLANES, :] = h * gate[lo:lo + SUBLANES, :]
        carry = jnp.broadcast_to(h[SUBLANES - 1:SUBLANES, :], (SUBLANES, c))
    hc_ref[...] = carry

    @pl.when(t == nt - 1)
    def _():
        cnew_ref[...] = ext_ref[0:pad, :]
        hnew_ref[...] = carry[0:1, :]


def _lru(u, cbuf8, h0, cw, cb, wg, bg, lam, tt, hp):
    b, t, c2 = u.shape
    c = c2 // 2
    fixed = lambda i, j: (0, 0)
    per_b = lambda i, j: (i, 0, 0)
    return pl.pallas_call(
        functools.partial(_lru_kernel, hp=hp),
        grid=(b, t // tt),
        in_specs=[
            pl.BlockSpec((None, tt, c2), lambda i, j: (i, j, 0)),
            pl.BlockSpec((None, SUBLANES, c), per_b),
            pl.BlockSpec((None, 1, c), per_b),
            pl.BlockSpec((CONV_W, c), fixed),
            pl.BlockSpec((1, c), fixed),
            pl.BlockSpec((c, c2), fixed),
            pl.BlockSpec((1, c2), fixed),
            pl.BlockSpec((1, c), fixed),
        ],
        out_specs=[
            pl.BlockSpec((None, tt, c), lambda i, j: (i, j, 0)),
            pl.BlockSpec((None, SUBLANES, c), per_b),
            pl.BlockSpec((None, 1, c), per_b),
        ],
        out_shape=[
            jax.ShapeDtypeStruct((b, t, c), F32),
            jax.ShapeDtypeStruct((b, SUBLANES, c), F32),
            jax.ShapeDtypeStruct((b, 1, c), F32),
        ],
        scratch_shapes=[pltpu.VMEM((tt + SUBLANES, c), F32), pltpu.VMEM((SUBLANES, c), F32)],
        compiler_params=pltpu.CompilerParams(dimension_semantics=("arbitrary", "arbitrary")),
        name="rg_lru",
    )(u, cbuf8, h0, cw, cb, wg, bg, lam)


def _mlstm_kernel(z_ref, act_ref, cum_ref, actT_ref, cumT_ref, c0_ref, n0_ref, m0_ref,
                  y_ref, c_ref, n_ref, m_ref, *, heads, tv, hp):
    ck = pl.program_id(1)
    L = act_ref.shape[0]
    dh = HEAD_DIM
    w = heads * dh

    @pl.when(ck == 0)
    def _():
        c_ref[...] = c0_ref[...]
        n_ref[...] = n0_ref[...]
        m_ref[...] = m0_ref[...]

    t_idx = lax.broadcasted_iota(jnp.int32, (L, L), 0)
    s_idx = lax.broadcasted_iota(jnp.int32, (L, L), 1)
    causal = s_idx <= t_idx
    col_ok = lax.broadcasted_iota(jnp.int32, (L, 1), 0) < tv
    row_ok = lax.broadcasted_iota(jnp.int32, (1, L), 1) < tv

    def rows(ref, lo):
        x = ref[:, lo:lo + dh]
        if tv < L:
            x = jnp.concatenate([x, jnp.zeros((L - tv, dh), F32)], axis=0)
        return x

    for h in range(heads):
        q = rows(z_ref, h * dh)
        k = rows(z_ref, w + h * dh) * (dh ** -0.5)
        v = rows(z_ref, 2 * w + h * dh)
        ic_col = act_ref[:, h:h + 1]
        bc_col = cum_ref[:, heads + h:heads + h + 1]
        ic_row = actT_ref[h:h + 1, :]
        bc_row = cumT_ref[heads + h:heads + h + 1, :]
        bc_last = bc_col[tv - 1:tv, :]
        if tv < L:
            ic_col = jnp.where(col_ok, ic_col, NEG)
            ic_row = jnp.where(row_ok, ic_row, NEG)
            bc_col = jnp.where(col_ok, bc_col, bc_last)
            bc_row = jnp.where(row_ok, bc_row, bc_last)
        m_prev = m_ref[h]
        c_prev = c_ref[h]
        n_prev = n_ref[h]

        dlog = jnp.where(causal, bc_col - bc_row + ic_row, NEG)
        inter = bc_col + m_prev
        mt = jnp.maximum(inter, jnp.max(dlog, axis=1, keepdims=True))
        qb = q if hp else q.astype(BF16)
        vb = v if hp else v.astype(BF16)
        sqk = _mm(qb, k, hp, nt=True) * jnp.exp(dlog - mt)
        decay = jnp.exp(inter - mt)
        num = _mm(sqk, vb, hp) + decay * _mm(qb, c_prev, hp)
        den = jnp.sum(sqk, axis=1, keepdims=True) + decay * jnp.sum(q * n_prev, axis=1, keepdims=True)
        hh = num / jnp.maximum(jnp.abs(den), jnp.exp(-mt))
        m_new = mt[L - 1:L, :]
        wk = jnp.exp(bc_last - bc_col + ic_col - m_new)
        g = jnp.exp(bc_last + m_prev - m_new)
        kw = k * wk
        c_ref[h] = g * c_prev + _mm(kw.T, vb, hp)
        n_ref[h] = g * n_prev + jnp.sum(kw, axis=0, keepdims=True)
        m_ref[h] = m_new
        o = z_ref[:, 3 * w + h * dh:3 * w + (h + 1) * dh]
        y_ref[:, h * dh:(h + 1) * dh] = jax.nn.sigmoid(o) * hh[0:tv, :]


def _mlstm(z, act, cum, actT, cumT, c0, n0, m0, heads, hp):
    b, t, w4 = z.shape
    w = w4 // 4
    L = CHUNK
    tv = min(t, L)
    nc = act.shape[1] // L
    gw = act.shape[-1]
    rows = actT.shape[1]
    dh = HEAD_DIM
    tok = lambda i, c: (i, c, 0)
    tokT = lambda i, c: (i, 0, c)
    st = lambda i, c: (i, 0, 0, 0)
    st_specs = [pl.BlockSpec((None, heads, dh, dh), st), pl.BlockSpec((None, heads, 1, dh), st),
                pl.BlockSpec((None, heads, 1, 1), st)]
    return pl.pallas_call(
        functools.partial(_mlstm_kernel, heads=heads, tv=tv, hp=hp),
        grid=(b, nc),
        in_specs=[pl.BlockSpec((None, tv, w4), tok), pl.BlockSpec((None, L, gw), tok), pl.BlockSpec((None, L, gw), tok),
                  pl.BlockSpec((None, rows, L), tokT), pl.BlockSpec((None, rows, L), tokT)] + st_specs,
        out_specs=[pl.BlockSpec((None, tv, w), tok)] + st_specs,
        out_shape=[jax.ShapeDtypeStruct((b, t, w), F32), jax.ShapeDtypeStruct(c0.shape, F32),
                   jax.ShapeDtypeStruct(n0.shape, F32), jax.ShapeDtypeStruct(m0.shape, F32)],
        compiler_params=pltpu.CompilerParams(dimension_semantics=("arbitrary", "arbitrary")),
        name="mlstm",
    )(z, act, cum, actT, cumT, c0, n0, m0)


def _fox_prompt_kernel(q_ref, k_ref, v_ref, f_ref, y_ref):
    i = pl.program_id(2)
    tq = q_ref.shape[0]
    tk = tq
    dh = HEAD_DIM
    q2 = q_ref[...]
    lane = lax.broadcasted_iota(jnp.int32, q2.shape, 1)
    qm = [jnp.where(lane < dh, q2, jnp.zeros_like(q2)), jnp.where(lane >= dh, q2, jnp.zeros_like(q2))]

    def block(j, carry, diagonal):
        start = pl.multiple_of(j * tk, tk)
        kb = k_ref[:, pl.ds(start, tk)]
        vb = v_ref[:, pl.ds(start, tk)]
        out = []
        for hh in range(2):
            m, l, acc = carry[hh]
            x = _dot(qm[hh], kb) - f_ref[hh:hh + 1, pl.ds(start, tk)]
            if diagonal:
                r = lax.broadcasted_iota(jnp.int32, x.shape, 0)
                c = lax.broadcasted_iota(jnp.int32, x.shape, 1)
                x = jnp.where(c <= r, x, NEG)
            m_new = jnp.maximum(m, jnp.max(x, axis=1, keepdims=True))
            alpha = jnp.exp(m - m_new)
            p = jnp.exp(x - m_new)
            l = alpha * l + jnp.sum(p, axis=1, keepdims=True)
            acc = alpha * acc + _dot_nt(p.astype(BF16), vb)
            out.append((m_new, l, acc))
        return tuple(out)

    init = tuple((jnp.full((tq, 1), NEG, F32), jnp.zeros((tq, 1), F32), jnp.zeros((tq, 2 * dh), F32))
                 for _ in range(2))
    carry = lax.fori_loop(0, i, lambda j, c: block(j, c, False), init)
    (_, l0, a0), (_, l1, a1) = block(i, carry, True)
    y_ref[...] = jnp.where(lane < dh, a0 / l0, a1 / l1)


def _fox_prompt(q, kt, vt, f_pairs, tq):
    b, s, w = q.shape
    pairs = w // LANES
    return pl.pallas_call(
        _fox_prompt_kernel,
        grid=(b, pairs, s // tq),
        in_specs=[
            pl.BlockSpec((None, tq, LANES), lambda bi, p, i: (bi, i, p)),
            pl.BlockSpec((None, LANES, s), lambda bi, p, i: (bi, p, 0)),
            pl.BlockSpec((None, LANES, s), lambda bi, p, i: (bi, p, 0)),
            pl.BlockSpec((None, None, 2, s), lambda bi, p, i: (bi, p, 0, 0)),
        ],
        out_specs=pl.BlockSpec((None, tq, LANES), lambda bi, p, i: (bi, i, p)),
        out_shape=jax.ShapeDtypeStruct((b, s, w), F32),
        compiler_params=pltpu.CompilerParams(dimension_semantics=("arbitrary", "arbitrary", "arbitrary")),
        name="fox_prompt",
    )(q, kt, vt, f_pairs)


def _fox_decode_kernel(pt_ref, q_ref, kn_ref, vn_ref, gt_ref, *rest, pps, heads, hp):
    k_refs = rest[0:pps]
    v_refs = rest[pps:2 * pps]
    lf_refs = rest[2 * pps:3 * pps]
    y_ref = rest[3 * pps]
    qbd_ref, m_ref, l_ref, acc_ref, car_ref = rest[3 * pps + 1:]
    j = pl.program_id(1)
    nj = pl.num_programs(1)
    t_new = q_ref.shape[0]
    w = q_ref.shape[1]
    dh = HEAD_DIM
    page = k_refs[0].shape[1]
    rows = heads * t_new
    lane_head = lax.broadcasted_iota(jnp.int32, (t_new, w), 1) // dh

    @pl.when(j == 0)
    def _():
        q = q_ref[...]
        for h in range(heads):
            qbd_ref[h * t_new:(h + 1) * t_new, :] = jnp.where(lane_head == h, q, jnp.zeros_like(q))
        m_ref[...] = jnp.full(m_ref.shape, NEG, F32)
        l_ref[...] = jnp.zeros_like(l_ref)
        acc_ref[...] = jnp.zeros_like(acc_ref)
        car_ref[...] = jnp.zeros_like(car_ref)

    qbd = qbd_ref[...]

    def expand(r):
        return jnp.concatenate([jnp.broadcast_to(r[h:h + 1, :], (t_new, r.shape[1])) for h in range(heads)], axis=0)

    lft = jnp.concatenate([lf_refs[r][...] for r in range(pps)], axis=0)
    jj = lax.broadcasted_iota(jnp.int32, (page, 2 * page), 0)
    ss = lax.broadcasted_iota(jnp.int32, (page, 2 * page), 1)
    u_aug = jnp.where((jj > ss) | (ss >= page), 1.0, 0.0).astype(F32)
    suf = _dot_hi(lft, u_aug)
    carry = car_ref[...]
    xs = []
    for r in range(pps):
        r_loc = suf[r * heads:(r + 1) * heads, 0:page] + carry
        carry = carry + suf[r * heads:(r + 1) * heads, page:2 * page]
        xs.append(_mm(qbd, k_refs[r][...], hp) + expand(r_loc))
    car_ref[...] = carry

    m_prev = m_ref[...]
    m_new = m_prev
    for x in xs:
        m_new = jnp.maximum(m_new, jnp.max(x, axis=1, keepdims=True))
    alpha = jnp.exp(m_prev - m_new)
    l = alpha * l_ref[...]
    acc = alpha * acc_ref[...]
    for r in range(pps):
        p = jnp.exp(xs[r] - m_new)
        l = l + jnp.sum(p, axis=1, keepdims=True)
        acc = acc + _mm(p, v_refs[r][...], hp, nt=True)
    m_ref[...] = m_new
    l_ref[...] = l
    acc_ref[...] = acc

    @pl.when(j == nj - 1)
    def _():
        x = _mm(qbd, kn_ref[...], hp, nt=True) - expand(gt_ref[...])
        tq = lax.broadcasted_iota(jnp.int32, (rows, t_new), 0) % t_new
        sk = lax.broadcasted_iota(jnp.int32, (rows, t_new), 1)
        x = jnp.where(sk <= tq, x, NEG)
        m_fin = jnp.maximum(m_new, jnp.max(x, axis=1, keepdims=True))
        a2 = jnp.exp(m_new - m_fin)
        p = jnp.exp(x - m_fin)
        l_fin = a2 * l + jnp.sum(p, axis=1, keepdims=True)
        o = (a2 * acc + _mm(p, vn_ref[...], hp)) / l_fin
        out = jnp.zeros((t_new, w), F32)
        for h in range(heads):
            out = jnp.where(lane_head == h, o[h * t_new:(h + 1) * t_new, :], out)
        y_ref[...] = out


def _fox_decode(page_table, q, k_new, v_new, g_t, cache_k, cache_v, cache_lf, layer, pps, hp):
    b, t_new, w = q.shape
    heads = cache_lf.shape[2]
    page = cache_k.shape[3]
    n_pages = page_table.shape[1]
    steps = n_pages // pps
    rows = heads * t_new

    def page_map(r):
        return lambda bi, j, pt: (layer, pt[bi, n_pages - 1 - (j * pps + r)], 0, 0)

    tok = lambda bi, j, pt: (bi, 0, 0)
    in_specs = [pl.BlockSpec((None, t_new, w), tok), pl.BlockSpec((None, t_new, w), tok),
                pl.BlockSpec((None, t_new, w), tok), pl.BlockSpec((None, heads, t_new), tok)]
    in_specs += [pl.BlockSpec((None, None, w, page), page_map(r)) for r in range(pps)]
    in_specs += [pl.BlockSpec((None, None, w, page), page_map(r)) for r in range(pps)]
    in_specs += [pl.BlockSpec((None, None, heads, page), page_map(r)) for r in range(pps)]
    grid_spec = pltpu.PrefetchScalarGridSpec(
        num_scalar_prefetch=1,
        grid=(b, steps),
        in_specs=in_specs,
        out_specs=pl.BlockSpec((None, t_new, w), tok),
        scratch_shapes=[pltpu.VMEM((rows, w), q.dtype), pltpu.VMEM((rows, 1), F32), pltpu.VMEM((rows, 1), F32),
                        pltpu.VMEM((rows, w), F32), pltpu.VMEM((heads, page), F32)],
    )
    return pl.pallas_call(
        functools.partial(_fox_decode_kernel, pps=pps, heads=heads, hp=hp),
        grid_spec=grid_spec,
        out_shape=jax.ShapeDtypeStruct((b, t_new, w), F32),
        compiler_params=pltpu.CompilerParams(dimension_semantics=("arbitrary", "arbitrary")),
        name="fox_decode",
    )(page_table, q, k_new, v_new, g_t, *([cache_k] * pps), *([cache_v] * pps), *([cache_lf] * pps))


ROUTE_W = LANES


def _route(logits):
    lane = lax.broadcasted_iota(jnp.int32, logits.shape, 1).astype(F32)
    big = float(ROUTE_W)

    def first_max(x):
        mx = jnp.max(x, axis=1, keepdims=True)
        return mx, jnp.min(jnp.where(x == mx, lane, big), axis=1, keepdims=True)

    gmask = lane < N_GROUPS
    lg = jnp.where(gmask, logits, NEG)
    mg, gidx = first_max(lg)
    pg_sel = 1.0 / jnp.sum(jnp.where(gmask, jnp.exp(lg - mg), 0.0), axis=1, keepdims=True)
    lo = N_GROUPS + gidx * EXPERTS_PER_GROUP
    le = jnp.where((lane >= lo) & (lane < lo + EXPERTS_PER_GROUP), logits, NEG)
    v1, i1 = first_max(le)
    le2 = jnp.where(lane == i1, NEG, le)
    v2, i2 = first_max(le2)
    e21 = jnp.exp(v2 - v1)
    w1 = pg_sel / (1.0 + e21)
    w2 = w1 * e21
    tile = jnp.where(lane == i1, w1, 0.0) + jnp.where(lane == i2, w2, 0.0)
    tile = jnp.where(lane == 0.0, i1 - N_GROUPS, tile)
    tile = jnp.where(lane == 1.0, i2 - N_GROUPS, tile)
    tile = jnp.where(lane == 2.0, w1, tile)
    tile = jnp.where(lane == 3.0, w2, tile)
    return tile


def _out_proj_kernel(x_ref, yl_ref, ym_ref, yf_ref, go_ref, wo_ref, g2_ref, wr_ref, br_ref,
                     xo_ref, xn_ref, rt_ref, *, hp):
    acc = x_ref[...]
    off = 0
    for y_ref in (yl_ref, ym_ref, yf_ref):
        wd = y_ref.shape[-1]
        yn = _rms(y_ref[...]) * go_ref[:, off:off + wd]
        acc = acc + _mm(yn, wo_ref[off:off + wd, :], hp)
        off += wd
    xo_ref[...] = acc
    xn = _rms(acc) * g2_ref[...]
    xn_ref[...] = xn.astype(BF16)
    rt_ref[...] = _route(_dot_hi(xn, wr_ref[...]) + br_ref[...])


def _out_proj(x, yl, ym, yf, g_out, w_out, g2, w_route, b_route, tm, hp):
    n, d = x.shape
    row = lambda i: (i, 0)
    fixed = lambda i: (0, 0)
    ins = (x, yl, ym, yf)
    return pl.pallas_call(
        functools.partial(_out_proj_kernel, hp=hp),
        grid=(n // tm,),
        in_specs=[pl.BlockSpec((tm, a.shape[1]), row) for a in ins] + [
            pl.BlockSpec((1, d), fixed), pl.BlockSpec(w_out.shape, fixed), pl.BlockSpec((1, d), fixed),
            pl.BlockSpec(w_route.shape, fixed), pl.BlockSpec((1, ROUTE_W), fixed)],
        out_specs=[pl.BlockSpec((tm, d), row), pl.BlockSpec((tm, d), row), pl.BlockSpec((tm, ROUTE_W), row)],
        out_shape=[jax.ShapeDtypeStruct((n, d), F32), jax.ShapeDtypeStruct((n, d), BF16),
                   jax.ShapeDtypeStruct((n, ROUTE_W), F32)],
        compiler_params=pltpu.CompilerParams(dimension_semantics=("arbitrary",)),
        name="out_proj",
    )(x, yl, ym, yf, g_out, w_out, g2, w_route, b_route)


def _moe_dense_kernel(x_ref, xn_ref, rt_ref, wg_ref, wu_ref, wd_ref, o_ref):
    e = pl.program_id(1)

    @pl.when(e == 0)
    def _():
        o_ref[...] = x_ref[...]

    xn = xn_ref[...]
    h = jax.nn.silu(_dot(xn, wg_ref[...])) * _dot(xn, wu_ref[...])
    rt = rt_ref[...]
    lane = lax.broadcasted_iota(jnp.int32, rt.shape, 1)
    gate = jnp.sum(jnp.where(lane == N_GROUPS + e, rt, 0.0), axis=1, keepdims=True)
    o_ref[...] += gate * _dot(h.astype(BF16), wd_ref[...])


def _moe_dense(x, xn, rt, wg, wu, wd, tm):
    n, d = x.shape
    ne, _, ff = wg.shape
    row = lambda i, e: (i, 0)
    return pl.pallas_call(
        _moe_dense_kernel,
        grid=(n // tm, ne),
        in_specs=[pl.BlockSpec((tm, d), row), pl.BlockSpec((tm, d), row), pl.BlockSpec((tm, ROUTE_W), row),
                  pl.BlockSpec((None, d, ff), lambda i, e: (e, 0, 0)),
                  pl.BlockSpec((None, d, ff), lambda i, e: (e, 0, 0)),
                  pl.BlockSpec((None, ff, d), lambda i, e: (e, 0, 0))],
        out_specs=pl.BlockSpec((tm, d), row),
        out_shape=jax.ShapeDtypeStruct((n, d), F32),
        compiler_params=pltpu.CompilerParams(dimension_semantics=("arbitrary", "arbitrary")),
        name="moe_dense",
    )(x, xn, rt, wg, wu, wd)


def _final_norm_kernel(x_ref, g_ref, o_ref):
    o_ref[...] = _rms(x_ref[...]) * g_ref[...]


def _final_norm(x, g, tm):
    n, d = x.shape
    return pl.pallas_call(
        _final_norm_kernel,
        grid=(n // tm,),
        in_specs=[pl.BlockSpec((tm, d), lambda i: (i, 0)), pl.BlockSpec((1, d), lambda i: (0, 0))],
        out_specs=pl.BlockSpec((tm, d), lambda i: (i, 0)),
        out_shape=jax.ShapeDtypeStruct((n, d), F32),
        compiler_params=pltpu.CompilerParams(dimension_semantics=("arbitrary",)),
        name="final_norm",
    )(x, g.reshape(1, d))


def _perm_w_in(w_in, dims, m_heads, f_heads, kv_t, dtype):
    lw, mw, fw = dims
    o_g = 2 * lw + 4 * mw
    o_q = o_g + 2 * m_heads
    o_k = o_q + fw
    o_ff = o_k + 2 * fw
    n_gate = 2 * m_heads + f_heads
    gates = jnp.concatenate([w_in[:, o_g:o_q], w_in[:, o_ff:o_ff + f_heads]], axis=1)
    gates = jnp.pad(gates, ((0, 0), (0, GATE_W - n_gate)))
    main = jnp.concatenate([w_in[:, :o_g], w_in[:, o_q:o_k], gates], axis=1).astype(dtype)
    kv = w_in[:, o_k:o_ff].astype(dtype)
    return main, (kv.T if kv_t else kv)


def _route_weights(rg, rgb, re, reb):
    n = rg.shape[1] + re.shape[1]
    w = jnp.pad(jnp.concatenate([rg, re], axis=1), ((0, 0), (0, ROUTE_W - n)))
    b = jnp.pad(jnp.concatenate([rgb, reb]), (0, ROUTE_W - n)).reshape(1, ROUTE_W)
    return w, b


def _lru_gate_weights(wr, wi, dtype):
    nb, bs, _ = wr.shape
    eye = jnp.eye(nb, dtype=wr.dtype)

    def dense(w):
        return jnp.einsum("ncd,nm->ncmd", w, eye).reshape(nb * bs, nb * bs)

    return jnp.concatenate([dense(wr), dense(wi)], axis=1).astype(dtype)


def _tile(n, preferred):
    return preferred if n % preferred == 0 else n


def kernel(x_prompt, x_sample, cache_k, cache_v, cache_logf, page_table, state_conv, state_lru_h, state_mlstm_C, state_mlstm_n, state_mlstm_m, norm1, w_in, conv_w, conv_b, lru_wr, lru_br, lru_wi, lru_bi, lru_lambda, m_bi, m_bf, fox_bf, out_norm, w_out, norm2, router_g, router_g_b, router_e, router_e_b, w_gate, w_up, w_down, final_norm):
    depth, d = norm1.shape
    lw = conv_w.shape[-1]
    mh = m_bi.shape[-1]
    fh = fox_bf.shape[-1]
    dh = HEAD_DIM
    mw, fw = mh * dh, fh * dh
    dims = (lw, mw, fw)
    n_gate = 2 * mh + fh
    pool, page = cache_k.shape[1], cache_k.shape[2]
    ck = cache_k.transpose(0, 1, 3, 4, 2).reshape(depth, pool, fw, page)
    cv = cache_v.transpose(0, 1, 3, 4, 2).reshape(depth, pool, fw, page)
    clf = cache_logf.transpose(0, 1, 3, 2)

    def layer(x, l, conv_buf, h0, c0, n0, m0, prompt):
        b, t, _ = x.shape
        n = b * t
        x2 = x.reshape(n, d)
        hp = (not prompt) and l < depth - 1
        wdt = F32 if hp else BF16
        w_main, w_kv = _perm_w_in(w_in[l], dims, mh, fh, prompt, wdt)
        lru, mz, fqb, gate, *kv = _in_proj(x2, norm1[l], w_main, w_kv, dims, _tile(n, 512), prompt, t, hp)

        tp = -(-t // CHUNK) * CHUNK
        gate3 = gate.reshape(b, t, GATE_W)
        if tp != t:
            gate3 = jnp.pad(gate3, ((0, 0), (0, tp - t), (0, 0)))
        gbias = jnp.pad(jnp.concatenate([m_bi[l], m_bf[l], fox_bf[l]]), (0, GATE_W - n_gate)).reshape(1, GATE_W)
        act, cum, act_t, cum_t = _gates(gate3, gbias, mh, 2 * mh, 2 * SUBLANES)

        cbuf8 = jnp.pad(conv_buf, ((0, 0), (SUBLANES - (CONV_W - 1), 0), (0, 0)))
        wg = _lru_gate_weights(lru_wr[l], lru_wi[l], wdt)
        bg = jnp.concatenate([lru_br[l], lru_bi[l]]).reshape(1, 2 * lw)
        y_l, cn8, hn = _lru(lru.reshape(b, t, 2 * lw), cbuf8, h0.reshape(b, 1, lw), conv_w[l],
                            conv_b[l].reshape(1, lw), wg, bg, lru_lambda[l].reshape(1, lw), _tile(t, 256), hp)

        y_m, c_new, n_new, m_new = _mlstm(mz.reshape(b, t, 4 * mw), act, cum, act_t, cum_t, c0,
                                          n0.reshape(b, mh, 1, dh), m0.reshape(b, mh, 1, 1), mh, hp)

        f_t = cum_t[:, 2 * mh:2 * mh + fh, :]
        qb = fqb.reshape(b, t, fw)
        if prompt:
            kt, vt, ktb, vtb = kv
            y_f = _fox_prompt(qb, ktb, vtb, f_t.reshape(b, fw // LANES, 2, t), _tile(t, 256))
            k_state = kt.reshape(b, fh, dh, t).transpose(0, 3, 1, 2)
            v_state = vt.reshape(b, fh, dh, t).transpose(0, 3, 1, 2)
        else:
            k_new, v_new = kv
            y_f = _fox_decode(page_table, qb, k_new.reshape(b, t, fw), v_new.reshape(b, t, fw), f_t[:, :, :t],
                              ck, cv, clf, l, 8, hp)
            k_state = k_new.reshape(b, t, fh, dh)
            v_state = v_new.reshape(b, t, fh, dh)

        w_route, b_route = _route_weights(router_g[l], router_g_b[l], router_e[l], router_e_b[l])
        x1, xn, rt = _out_proj(x2, y_l.reshape(n, lw), y_m.reshape(n, mw), y_f.reshape(n, fw),
                               out_norm[l].reshape(1, d), w_out[l].astype(wdt), norm2[l].reshape(1, d),
                               w_route, b_route, _tile(n, 512), hp)
        x_next = _moe_dense(x1, xn, rt, w_gate[l].astype(BF16), w_up[l].astype(BF16), w_down[l].astype(BF16),
                            _tile(n, 1024))
        lf_state = act_t[:, 2 * mh:2 * mh + fh, :t].transpose(0, 2, 1)
        state = (k_state, v_state, lf_state, cn8[:, SUBLANES - (CONV_W - 1):], hn[:, 0], c_new,
                 n_new.reshape(b, mh, dh), m_new.reshape(b, mh))
        return x_next.reshape(b, t, d), state

    bp = x_prompt.shape[0]
    zc = jnp.zeros((bp, CONV_W - 1, lw), F32)
    zh = jnp.zeros((bp, lw), F32)
    z_c = jnp.zeros((bp, mh, dh, dh), F32)
    zn = jnp.zeros((bp, mh, dh), F32)
    zm = jnp.zeros((bp, mh), F32)

    xp, xs = x_prompt, x_sample
    st_p, st_s = [], []
    for l in range(depth):
        xp, sp = layer(xp, l, zc, zh, z_c, zn, zm, True)
        xs, ss = layer(xs, l, state_conv[l], state_lru_h[l], state_mlstm_C[l], state_mlstm_n[l],
                       state_mlstm_m[l], False)
        st_p.append(sp)
        st_s.append(ss)

    def fin(x):
        b, t, _ = x.shape
        return _final_norm(x.reshape(b * t, d), final_norm, _tile(b * t, 512)).reshape(b, t, d)

    def stk(states, i):
        return jnp.stack([s[i] for s in states])

    return ((fin(xp), fin(xs)) + tuple(stk(st_p, i) for i in range(8)) + tuple(stk(st_s, i) for i in range(8)))
```

```python
import functools
import math

import jax
import jax.numpy as jnp
from jax import lax
from jax.experimental import pallas as pl
from jax.experimental.pallas import tpu as pltpu

F32 = jnp.float32
BF16 = jnp.bfloat16

HEAD_DIM = 64
LRU_C = 8.0
CONV_W = 4
CHUNK = 128
N_GROUPS = 4
EXPERTS_PER_GROUP = 4
EPS = 1e-6
NEG = -1e30
LANES = 128
SUBLANES = 8
GATE_W = LANES

LOG2E = 1.4426950408889634


def _dot(a, b):
    return jnp.dot(a, b, preferred_element_type=F32)


def _dot_nt(a, b):
    return lax.dot_general(a, b, (((1,), (1,)), ((), ())), preferred_element_type=F32)


def _split(a):
    hi = a.astype(BF16)
    return hi, (a - hi.astype(F32)).astype(BF16)


def _mm(a, b, hp, nt=False):
    dot = _dot_nt if nt else _dot
    if not hp:
        return dot(a.astype(BF16), b.astype(BF16))
    ah, al = _split(a)
    bh, bl = _split(b)
    return dot(ah, bh) + (dot(al, bh) + dot(ah, bl))


def _mm_sel(a, sel, sel_left=False):
    hi = a.astype(BF16)
    r1 = a - hi.astype(F32)
    mid = r1.astype(BF16)
    lo = (r1 - mid.astype(F32)).astype(BF16)
    sel = sel.astype(BF16)
    if sel_left:
        return _dot(sel, hi) + (_dot(sel, mid) + _dot(sel, lo))
    return _dot(hi, sel) + (_dot(mid, sel) + _dot(lo, sel))


def _log_sigmoid(x):
    return jnp.minimum(x, 0.0) - jnp.log1p(jnp.exp(-jnp.abs(x)))


def _softplus(x):
    return jnp.maximum(x, 0.0) + jnp.log1p(jnp.exp(-jnp.abs(x)))


def _gelu_tanh(x):
    return 0.5 * x * (1.0 + jnp.tanh(math.sqrt(2.0 / math.pi) * (x + 0.044715 * (x * x * x))))


def _rms(x):
    return x * lax.rsqrt(jnp.mean(x * x, axis=-1, keepdims=True) + EPS)


def _in_proj_kernel(x_ref, g_ref, w_ref, *rest, row_spec, t_spec, hp):
    if t_spec:
        wt_ref, outs = rest[0], rest[1:]
    else:
        wt_ref, outs = None, rest
    xn = _rms(x_ref[...]) * g_ref[...]
    if not hp:
        xn = xn.astype(BF16)
    k = 0
    off = 0
    for width, scale in row_spec:
        r = _mm(xn, w_ref[:, off:off + width], hp)
        off += width
        if scale != 1.0:
            r = r * scale
        outs[k][...] = r.astype(outs[k].dtype)
        k += 1
    off = 0
    for rows, scale, copies in t_spec:
        r = _mm(wt_ref[off:off + rows, :], xn, hp, nt=True)
        off += rows
        if scale != 1.0:
            r = r * scale
        for _ in range(copies):
            outs[k][...] = r.astype(outs[k].dtype)
            k += 1


def _in_proj(x, g, w, w_t, row_outs, t_outs, tm, seq, hp):
    n, d = x.shape
    row = lambda i: (i, 0)
    fixed = lambda i: (0, 0)
    per_b = seq // tm
    tmap = lambda i: (i // per_b, 0, i % per_b)
    out_specs = [pl.BlockSpec((tm, wd), row) for wd, _, _ in row_outs]
    out_shape = [jax.ShapeDtypeStruct((n, wd), dt) for wd, dt, _ in row_outs]
    ins = [x, g.reshape(1, d), w]
    in_specs = [pl.BlockSpec((tm, d), row), pl.BlockSpec((1, d), fixed), pl.BlockSpec(w.shape, fixed)]
    if t_outs:
        ins.append(w_t)
        in_specs.append(pl.BlockSpec(w_t.shape, fixed))
    for rows, dts, _ in t_outs:
        out_specs += [pl.BlockSpec((None, rows, tm), tmap)] * len(dts)
        out_shape += [jax.ShapeDtypeStruct((n // seq, rows, seq), dt) for dt in dts]
    return pl.pallas_call(
        functools.partial(_in_proj_kernel, row_spec=tuple((wd, sc) for wd, _, sc in row_outs),
                          t_spec=tuple((rows, sc, len(dts)) for rows, dts, sc in t_outs), hp=hp),
        grid=(n // tm,),
        in_specs=in_specs,
        out_specs=out_specs,
        out_shape=out_shape,
        compiler_params=pltpu.CompilerParams(dimension_semantics=("arbitrary",)),
        name="in_proj",
    )(*ins)


def _gates_kernel(z_ref, b_ref, act_ref, cum_ref, actT_ref, cumT_ref, carry_ref, *, n_in, n_local):
    c = pl.program_id(1)
    L = CHUNK
    rows = actT_ref.shape[0]

    @pl.when(c == 0)
    def _():
        carry_ref[...] = jnp.zeros_like(carry_ref)

    r = lax.broadcasted_iota(jnp.int32, (L, L), 0)
    s = lax.broadcasted_iota(jnp.int32, (L, L), 1)
    tri = jnp.where(s <= r, 1.0, 0.0)
    carry = carry_ref[0:1, :]
    for u in range(z_ref.shape[0] // L):
        lo = u * L
        pre = z_ref[lo:lo + L, :] + b_ref[...]
        lane = lax.broadcasted_iota(jnp.int32, pre.shape, 1)
        act = jnp.where(lane < n_in, pre, _log_sigmoid(pre))
        cum = _mm_sel(act, tri, sel_left=True) + jnp.where(lane[0:1, :] >= n_local, carry, 0.0)
        carry = cum[L - 1:L, :]
        act_ref[lo:lo + L, :] = act
        cum_ref[lo:lo + L, :] = cum
        actT_ref[:, lo:lo + L] = act.T[0:rows, :]
        cumT_ref[:, lo:lo + L] = cum.T[0:rows, :]
    carry_ref[...] = jnp.broadcast_to(carry, carry_ref.shape)


def _gates(z, bias, n_in, n_local, rows, tt):
    b, t, gw = z.shape
    blk = pl.BlockSpec((None, tt, gw), lambda i, c: (i, c, 0))
    blk_t = pl.BlockSpec((None, rows, tt), lambda i, c: (i, 0, c))
    return pl.pallas_call(
        functools.partial(_gates_kernel, n_in=n_in, n_local=n_local),
        grid=(b, t // tt),
        in_specs=[blk, pl.BlockSpec((1, gw), lambda i, c: (0, 0))],
        out_specs=[blk, blk, blk_t, blk_t],
        out_shape=[jax.ShapeDtypeStruct((b, t, gw), F32)] * 2 + [jax.ShapeDtypeStruct((b, rows, t), F32)] * 2,
        scratch_shapes=[pltpu.VMEM((SUBLANES, gw), F32)],
        compiler_params=pltpu.CompilerParams(dimension_semantics=("arbitrary", "arbitrary")),
        name="gates",
    )(z, bias)


def _lru_kernel(u_ref, cbuf_ref, h0_ref, cw_ref, cb_ref, wg_ref, bg_ref, lam_ref,
                y_ref, cnew_ref, hnew_ref, ext_ref, hc_ref, *, hp):
    t = pl.program_id(1)
    nt = pl.num_programs(1)
    tt = y_ref.shape[0]
    c = y_ref.shape[1]
    pad = SUBLANES

    @pl.when(t == 0)
    def _():
        ext_ref[0:pad, :] = cbuf_ref[...]
        hc_ref[...] = jnp.broadcast_to(h0_ref[...], hc_ref.shape)

    u = u_ref[:, 0:c]
    gl = u_ref[:, c:2 * c]
    ext_ref[pad:pad + tt, :] = u
    ext = ext_ref[...]
    xc = cw_ref[CONV_W - 1:CONV_W, :] * u + cb_ref[...]
    for j in range(CONV_W - 1):
        xc = xc + cw_ref[j:j + 1, :] * pltpu.roll(ext, CONV_W - 1 - j, 0)[pad:pad + tt, :]
    ext_ref[0:pad, :] = ext_ref[tt:tt + pad, :]

    pre = _mm(xc, wg_ref[...], hp) + bg_ref[...]
    r = jax.nn.sigmoid(pre[:, 0:c])
    i = jax.nn.sigmoid(pre[:, c:2 * c])
    log_a = (-LRU_C) * r * _softplus(-lam_ref[...])
    a = jnp.exp(log_a)
    th = jnp.tanh(log_a)
    bx = jnp.sqrt(-2.0 * th / (1.0 - th)) * (i * xc)

    row = lax.broadcasted_iota(jnp.int32, (tt, c), 0) & (SUBLANES - 1)
    s = 1
    while s < SUBLANES:
        keep = row >= s
        bx = jnp.where(keep, a * pltpu.roll(bx, s, 0) + bx, bx)
        a = jnp.where(keep, a * pltpu.roll(a, s, 0), a)
        s *= 2
    carry = hc_ref[...]
    gate = _gelu_tanh(gl)
    for g in range(tt // SUBLANES):
        lo = g * SUBLANES
        h = a[lo:lo + SUBLANES, :] * carry + bx[lo:lo + SUBLANES, :]
        y_ref[lo:lo + SUBLANES, :] = h * gate[lo:lo + SUBLANES, :]
        carry = jnp.broadcast_to(h[SUBLANES - 1:SUBLANES, :], (SUBLANES, c))
    hc_ref[...] = carry

    @pl.when(t == nt - 1)
    def _():
        cnew_ref[...] = ext_ref[0:pad, :]
        hnew_ref[...] = carry[0:1, :]


def _lru(u, cbuf8, h0, cw, cb, wg, bg, lam, tt, hp):
    b, t, c2 = u.shape
    c = c2 // 2
    fixed = lambda i, j: (0, 0)
    per_b = lambda i, j: (i, 0, 0)
    return pl.pallas_call(
        functools.partial(_lru_kernel, hp=hp),
        grid=(b, t // tt),
        in_specs=[
            pl.BlockSpec((None, tt, c2), lambda i, j: (i, j, 0)),
            pl.BlockSpec((None, SUBLANES, c), per_b),
            pl.BlockSpec((None, 1, c), per_b),
            pl.BlockSpec((CONV_W, c), fixed),
            pl.BlockSpec((1, c), fixed),
            pl.BlockSpec((c, c2), fixed),
            pl.BlockSpec((1, c2), fixed),
            pl.BlockSpec((1, c), fixed),
        ],
        out_specs=[
            pl.BlockSpec((None, tt, c), lambda i, j: (i, j, 0)),
            pl.BlockSpec((None, SUBLANES, c), per_b),
            pl.BlockSpec((None, 1, c), per_b),
        ],
        out_shape=[
            jax.ShapeDtypeStruct((b, t, c), F32),
            jax.ShapeDtypeStruct((b, SUBLANES, c), F32),
            jax.ShapeDtypeStruct((b, 1, c), F32),
        ],
        scratch_shapes=[pltpu.VMEM((tt + SUBLANES, c), F32), pltpu.VMEM((SUBLANES, c), F32)],
        compiler_params=pltpu.CompilerParams(dimension_semantics=("arbitrary", "arbitrary")),
        name="rg_lru",
    )(u, cbuf8, h0, cw, cb, wg, bg, lam)


def _mlstm_kernel(z_ref, act_ref, cum_ref, actT_ref, cumT_ref, c0_ref, n0_ref, m0_ref,
                  y_ref, c_ref, n_ref, m_ref, *, heads, tv, hp):
    ck = pl.program_id(1)
    L = act_ref.shape[0]
    dh = HEAD_DIM
    w = heads * dh

    @pl.when(ck == 0)
    def _():
        c_ref[...] = c0_ref[...]
        n_ref[...] = n0_ref[...]
        m_ref[...] = m0_ref[...]

    t_idx = lax.broadcasted_iota(jnp.int32, (L, L), 0)
    s_idx = lax.broadcasted_iota(jnp.int32, (L, L), 1)
    causal = s_idx <= t_idx
    col_ok = lax.broadcasted_iota(jnp.int32, (L, 1), 0) < tv
    row_ok = lax.broadcasted_iota(jnp.int32, (1, L), 1) < tv

    def rows(ref, lo):
        x = ref[:, lo:lo + dh]
        if tv < L:
            x = jnp.concatenate([x, jnp.zeros((L - tv, dh), F32)], axis=0)
        return x

    for h in range(heads):
        q = rows(z_ref, h * dh)
        k = rows(z_ref, w + h * dh) * (dh ** -0.5)
        v = rows(z_ref, 2 * w + h * dh)
        ic_col = act_ref[:, h:h + 1]
        bc_col = cum_ref[:, heads + h:heads + h + 1]
        ic_row = actT_ref[h:h + 1, :]
        bc_row = cumT_ref[heads + h:heads + h + 1, :]
        bc_last = bc_col[tv - 1:tv, :]
        if tv < L:
            ic_col = jnp.where(col_ok, ic_col, NEG)
            ic_row = jnp.where(row_ok, ic_row, NEG)
            bc_col = jnp.where(col_ok, bc_col, bc_last)
            bc_row = jnp.where(row_ok, bc_row, bc_last)
        m_prev = m_ref[h]
        c_prev = c_ref[h]
        n_prev = n_ref[h]

        dlog = jnp.where(causal, bc_col - bc_row + ic_row, NEG)
        inter = bc_col + m_prev
        mt = jnp.maximum(inter, jnp.max(dlog, axis=1, keepdims=True))
        qb = q if hp else q.astype(BF16)
        vb = v if hp else v.astype(BF16)
        sqk = _mm(qb, k, hp, nt=True) * jnp.exp(dlog - mt)
        decay = jnp.exp(inter - mt)
        num = _mm(sqk, vb, hp) + decay * _mm(qb, c_prev, hp)
        den = jnp.sum(sqk, axis=1, keepdims=True) + decay * jnp.sum(q * n_prev, axis=1, keepdims=True)
        hh = num / jnp.maximum(jnp.abs(den), jnp.exp(-mt))
        m_new = mt[L - 1:L, :]
        wk = jnp.exp(bc_last - bc_col + ic_col - m_new)
        g = jnp.exp(bc_last + m_prev - m_new)
        kw = k * wk
        c_ref[h] = g * c_prev + _mm(kw.T, vb, hp)
        n_ref[h] = g * n_prev + jnp.sum(kw, axis=0, keepdims=True)
        m_ref[h] = m_new
        o = z_ref[:, 3 * w + h * dh:3 * w + (h + 1) * dh]
        y_ref[:, h * dh:(h + 1) * dh] = jax.nn.sigmoid(o) * hh[0:tv, :]


def _mlstm(z, act, cum, actT, cumT, c0, n0, m0, heads, hp):
    b, t, w4 = z.shape
    w = w4 // 4
    L = CHUNK
    tv = min(t, L)
    nc = act.shape[1] // L
    gw = act.shape[-1]
    rows = actT.shape[1]
    dh = HEAD_DIM
    tok = lambda i, c: (i, c, 0)
    tokT = lambda i, c: (i, 0, c)
    st = lambda i, c: (i, 0, 0, 0)
    st_specs = [pl.BlockSpec((None, heads, dh, dh), st), pl.BlockSpec((None, heads, 1, dh), st),
                pl.BlockSpec((None, heads, 1, 1), st)]
    return pl.pallas_call(
        functools.partial(_mlstm_kernel, heads=heads, tv=tv, hp=hp),
        grid=(b, nc),
        in_specs=[pl.BlockSpec((None, tv, w4), tok), pl.BlockSpec((None, L, gw), tok), pl.BlockSpec((None, L, gw), tok),
                  pl.BlockSpec((None, rows, L), tokT), pl.BlockSpec((None, rows, L), tokT)] + st_specs,
        out_specs=[pl.BlockSpec((None, tv, w), tok)] + st_specs,
        out_shape=[jax.ShapeDtypeStruct((b, t, w), F32), jax.ShapeDtypeStruct(c0.shape, F32),
                   jax.ShapeDtypeStruct(n0.shape, F32), jax.ShapeDtypeStruct(m0.shape, F32)],
        compiler_params=pltpu.CompilerParams(dimension_semantics=("arbitrary", "arbitrary")),
        name="mlstm",
    )(z, act, cum, actT, cumT, c0, n0, m0)


def _fox_prompt_kernel(qt_ref, k_ref, vt_ref, cum_ref, y_ref, fb_ref, xs_ref, ml_ref, acc_ref, *, f_lane0, tk):
    i = pl.program_id(1)
    w, tq = qt_ref.shape
    dh = HEAD_DIM
    heads = w // dh
    sub = tq // tk
    reps = tq // LANES

    @pl.when(i == 0)
    def _():
        cum = cum_ref[...]
        lane = lax.broadcasted_iota(jnp.int32, cum.shape, 1)
        for h in range(heads):
            col = jnp.sum(jnp.where(lane == f_lane0 + h, cum, 0.0), axis=1, keepdims=True)
            fb_ref[h] = jnp.broadcast_to(col * LOG2E, fb_ref.shape[1:])

    row = lax.broadcasted_iota(jnp.int32, (LANES, tq), 0)
    qm = []
    for h in range(heads):
        qt = qt_ref[(h // 2) * LANES:(h // 2 + 1) * LANES, :]
        keep = (row < dh) if h % 2 == 0 else (row >= dh)
        qm.append(jnp.where(keep, qt, jnp.zeros_like(qt)))

    def scores(start, slot):
        for h in range(heads):
            kb = k_ref[pl.ds(start, tk), (h // 2) * LANES:(h // 2 + 1) * LANES]
            fb = fb_ref[h, pl.ds(start, tk), :]
            xs_ref[slot, h] = _dot(kb, qm[h]) - jnp.concatenate([fb] * reps, axis=1)

    def update(slot, start, diag_off):
        for h in range(heads):
            x = xs_ref[slot, h]
            if diag_off is not None:
                r = lax.broadcasted_iota(jnp.int32, x.shape, 0) + diag_off
                c = lax.broadcasted_iota(jnp.int32, x.shape, 1)
                x = jnp.where(r <= c, x, NEG)
            m = ml_ref[h, 0:1, :]
            m_new = jnp.maximum(m, jnp.max(x, axis=0, keepdims=True))
            alpha = jnp.exp2(m - m_new)
            pr = jnp.exp2(x - m_new)
            ml_ref[h, 0:1, :] = m_new
            ml_ref[h, 1:2, :] = alpha * ml_ref[h, 1:2, :] + jnp.sum(pr, axis=0, keepdims=True)
            vb = vt_ref[h * dh:(h + 1) * dh, pl.ds(start, tk)]
            acc_ref[h] = alpha * acc_ref[h] + _dot(vb, pr.astype(BF16))

    for h in range(heads):
        ml_ref[h, 0:1, :] = jnp.full((1, tq), NEG, F32)
        ml_ref[h, 1:2, :] = jnp.zeros((1, tq), F32)
    acc_ref[...] = jnp.zeros_like(acc_ref)
    n = i * sub
    scores(0, 0)

    @pl.loop(0, n)
    def _(j):
        scores(pl.multiple_of((j + 1) * tk, tk), (j + 1) % 2)
        update(j % 2, pl.multiple_of(j * tk, tk), None)

    for u in range(sub):
        if u + 1 < sub:
            scores(pl.multiple_of(i * tq + (u + 1) * tk, tk), (n + u + 1) % 2)
        update((n + u) % 2, pl.multiple_of(i * tq + u * tk, tk), u * tk)
    y_ref[...] = jnp.concatenate([acc_ref[h] / ml_ref[h, 1:2, :] for h in range(heads)], axis=0).T


def _fox_prompt(qt, k, vt, cum, f_lane0, tq, tk):
    b, w, s = qt.shape
    heads = w // HEAD_DIM
    return pl.pallas_call(
        functools.partial(_fox_prompt_kernel, f_lane0=f_lane0, tk=tk),
        grid=(b, s // tq),
        in_specs=[
            pl.BlockSpec((None, w, tq), lambda bi, i: (bi, 0, i)),
            pl.BlockSpec((None, s, w), lambda bi, i: (bi, 0, 0)),
            pl.BlockSpec((None, w, s), lambda bi, i: (bi, 0, 0)),
            pl.BlockSpec((None, s, cum.shape[-1]), lambda bi, i: (bi, 0, 0)),
        ],
        out_specs=pl.BlockSpec((None, tq, w), lambda bi, i: (bi, i, 0)),
        out_shape=jax.ShapeDtypeStruct((b, s, w), F32),
        scratch_shapes=[pltpu.VMEM((heads, s, LANES), F32), pltpu.VMEM((2, heads, tk, tq), F32),
                        pltpu.VMEM((heads, SUBLANES, tq), F32), pltpu.VMEM((heads, HEAD_DIM, tq), F32)],
        compiler_params=pltpu.CompilerParams(dimension_semantics=("arbitrary", "arbitrary")),
        name="fox_prompt",
    )(qt, k, vt, cum)


def _fox_decode_kernel(pt_ref, q_ref, kn_ref, vn_ref, gt_ref, *rest, pps, heads, hp):
    k_refs = rest[0:pps]
    v_refs = rest[pps:2 * pps]
    lf_refs = rest[2 * pps:3 * pps]
    y_ref = rest[3 * pps]
    qh_ref, ql_ref, m_ref, l_ref, acc_ref, car_ref = rest[3 * pps + 1:]
    j = pl.program_id(1)
    nj = pl.num_programs(1)
    t_new = q_ref.shape[0]
    w = q_ref.shape[1]
    dh = HEAD_DIM
    page = k_refs[0].shape[1]
    rows = heads * t_new
    lane_head = lax.broadcasted_iota(jnp.int32, (t_new, w), 1) // dh

    @pl.when(j == 0)
    def _():
        q = q_ref[...].astype(F32)
        qh, ql = _split(q)
        for h in range(heads):
            keep = lane_head == h
            qh_ref[h * t_new:(h + 1) * t_new, :] = jnp.where(keep, qh, jnp.zeros_like(qh))
            ql_ref[h * t_new:(h + 1) * t_new, :] = jnp.where(keep, ql, jnp.zeros_like(ql))
        m_ref[...] = jnp.full(m_ref.shape, NEG, F32)
        l_ref[...] = jnp.zeros_like(l_ref)
        acc_ref[...] = jnp.zeros_like(acc_ref)
        car_ref[...] = jnp.zeros_like(car_ref)

    qh = qh_ref[...]
    ql = ql_ref[...]

    def qk(k, nt=False):
        dot = _dot_nt if nt else _dot
        if not hp:
            return dot(qh, k.astype(BF16))
        kh, kl = _split(k)
        return dot(qh, kh) + (dot(ql, kh) + dot(qh, kl))

    def expand(r):
        return jnp.concatenate([jnp.broadcast_to(r[h:h + 1, :], (t_new, r.shape[1])) for h in range(heads)], axis=0)

    lft = jnp.concatenate([lf_refs[r][...] for r in range(pps)], axis=0)
    jj = lax.broadcasted_iota(jnp.int32, (page, 2 * page), 0)
    ss = lax.broadcasted_iota(jnp.int32, (page, 2 * page), 1)
    u_aug = jnp.where((jj > ss) | (ss >= page), 1.0, 0.0)
    suf = _mm_sel(lft, u_aug)
    carry = car_ref[...]
    xs = []
    for r in range(pps):
        r_loc = suf[r * heads:(r + 1) * heads, 0:page] + carry
        carry = carry + suf[r * heads:(r + 1) * heads, page:2 * page]
        xs.append(qk(k_refs[r][...]) + expand(r_loc))
    car_ref[...] = carry

    m_prev = m_ref[...]
    m_new = m_prev
    for x in xs:
        m_new = jnp.maximum(m_new, jnp.max(x, axis=1, keepdims=True))
    alpha = jnp.exp(m_prev - m_new)
    l = alpha * l_ref[...]
    acc = alpha * acc_ref[...]
    for r in range(pps):
        p = jnp.exp(xs[r] - m_new)
        l = l + jnp.sum(p, axis=1, keepdims=True)
        acc = acc + _mm(p, v_refs[r][...], hp, nt=True)
    m_ref[...] = m_new
    l_ref[...] = l
    acc_ref[...] = acc

    @pl.when(j == nj - 1)
    def _():
        x = qk(kn_ref[...], nt=True) - expand(gt_ref[...])
        tq = lax.broadcasted_iota(jnp.int32, (rows, t_new), 0) % t_new
        sk = lax.broadcasted_iota(jnp.int32, (rows, t_new), 1)
        x = jnp.where(sk <= tq, x, NEG)
        m_fin = jnp.maximum(m_new, jnp.max(x, axis=1, keepdims=True))
        a2 = jnp.exp(m_new - m_fin)
        p = jnp.exp(x - m_fin)
        l_fin = a2 * l + jnp.sum(p, axis=1, keepdims=True)
        o = (a2 * acc + _mm(p, vn_ref[...], hp)) / l_fin
        out = jnp.zeros((t_new, w), F32)
        for h in range(heads):
            out = jnp.where(lane_head == h, o[h * t_new:(h + 1) * t_new, :], out)
        y_ref[...] = out


def _fox_decode(page_table, q, k_new, v_new, g_t, cache_k, cache_v, cache_lf, layer, pps, hp):
    b, t_new, w = q.shape
    heads = cache_lf.shape[2]
    page = cache_k.shape[3]
    n_pages = page_table.shape[1]
    steps = n_pages // pps
    rows = heads * t_new

    def page_map(r):
        return lambda bi, j, pt: (layer, pt[bi, n_pages - 1 - (j * pps + r)], 0, 0)

    tok = lambda bi, j, pt: (bi, 0, 0)
    in_specs = [pl.BlockSpec((None, t_new, w), tok), pl.BlockSpec((None, t_new, w), tok),
                pl.BlockSpec((None, t_new, w), tok), pl.BlockSpec((None, heads, t_new), tok)]
    in_specs += [pl.BlockSpec((None, None, w, page), page_map(r)) for r in range(pps)]
    in_specs += [pl.BlockSpec((None, None, w, page), page_map(r)) for r in range(pps)]
    in_specs += [pl.BlockSpec((None, None, heads, page), page_map(r)) for r in range(pps)]
    grid_spec = pltpu.PrefetchScalarGridSpec(
        num_scalar_prefetch=1,
        grid=(b, steps),
        in_specs=in_specs,
        out_specs=pl.BlockSpec((None, t_new, w), tok),
        scratch_shapes=[pltpu.VMEM((rows, w), BF16), pltpu.VMEM((rows, w), BF16),
                        pltpu.VMEM((rows, 1), F32), pltpu.VMEM((rows, 1), F32),
                        pltpu.VMEM((rows, w), F32), pltpu.VMEM((heads, page), F32)],
    )
    return pl.pallas_call(
        functools.partial(_fox_decode_kernel, pps=pps, heads=heads, hp=hp),
        grid_spec=grid_spec,
        out_shape=jax.ShapeDtypeStruct((b, t_new, w), F32),
        compiler_params=pltpu.CompilerParams(dimension_semantics=("arbitrary", "arbitrary")),
        name="fox_decode",
    )(page_table, q, k_new, v_new, g_t, *([cache_k] * pps), *([cache_v] * pps), *([cache_lf] * pps))


ROUTE_W = LANES


def _route(logits):
    lane = lax.broadcasted_iota(jnp.int32, logits.shape, 1).astype(F32)
    big = float(ROUTE_W)

    def first_max(x):
        mx = jnp.max(x, axis=1, keepdims=True)
        return mx, jnp.min(jnp.where(x == mx, lane, big), axis=1, keepdims=True)

    gmask = lane < N_GROUPS
    lg = jnp.where(gmask, logits, NEG)
    mg, gidx = first_max(lg)
    pg_sel = 1.0 / jnp.sum(jnp.where(gmask, jnp.exp(lg - mg), 0.0), axis=1, keepdims=True)
    lo = N_GROUPS + gidx * EXPERTS_PER_GROUP
    le = jnp.where((lane >= lo) & (lane < lo + EXPERTS_PER_GROUP), logits, NEG)
    v1, i1 = first_max(le)
    le2 = jnp.where(lane == i1, NEG, le)
    v2, i2 = first_max(le2)
    e21 = jnp.exp(v2 - v1)
    w1 = pg_sel / (1.0 + e21)
    w2 = w1 * e21
    tile = jnp.where(lane == i1, w1, 0.0) + jnp.where(lane == i2, w2, 0.0)
    tile = jnp.where(lane == 0.0, i1 - N_GROUPS, tile)
    tile = jnp.where(lane == 1.0, i2 - N_GROUPS, tile)
    tile = jnp.where(lane == 2.0, w1, tile)
    tile = jnp.where(lane == 3.0, w2, tile)
    return tile


def _out_proj_kernel(x_ref, yl_ref, ym_ref, yf_ref, go_ref, wo_ref, g2_ref, wr_ref, br_ref,
                     xo_ref, xn_ref, rt_ref, *, hp):
    acc = x_ref[...]
    off = 0
    for y_ref in (yl_ref, ym_ref, yf_ref):
        wd = y_ref.shape[-1]
        yn = _rms(y_ref[...]) * go_ref[:, off:off + wd]
        acc = acc + _mm(yn, wo_ref[off:off + wd, :], hp)
        off += wd
    xo_ref[...] = acc
    xn = _rms(acc) * g2_ref[...]
    xn_ref[...] = xn.astype(BF16)
    rt_ref[...] = _route(_mm(xn, wr_ref[...], True) + br_ref[...])


def _out_proj(x, yl, ym, yf, g_out, w_out, g2, w_route, b_route, tm, hp):
    n, d = x.shape
    row = lambda i: (i, 0)
    fixed = lambda i: (0, 0)
    ins = (x, yl, ym, yf)
    return pl.pallas_call(
        functools.partial(_out_proj_kernel, hp=hp),
        grid=(n // tm,),
        in_specs=[pl.BlockSpec((tm, a.shape[1]), row) for a in ins] + [
            pl.BlockSpec((1, d), fixed), pl.BlockSpec(w_out.shape, fixed), pl.BlockSpec((1, d), fixed),
            pl.BlockSpec(w_route.shape, fixed), pl.BlockSpec((1, ROUTE_W), fixed)],
        out_specs=[pl.BlockSpec((tm, d), row), pl.BlockSpec((tm, d), row), pl.BlockSpec((tm, ROUTE_W), row)],
        out_shape=[jax.ShapeDtypeStruct((n, d), F32), jax.ShapeDtypeStruct((n, d), BF16),
                   jax.ShapeDtypeStruct((n, ROUTE_W), F32)],
        compiler_params=pltpu.CompilerParams(dimension_semantics=("arbitrary",)),
        name="out_proj",
    )(x, yl, ym, yf, g_out, w_out, g2, w_route, b_route)


def _moe_dense_kernel(x_ref, xn_ref, rt_ref, wg_ref, wu_ref, wd_ref, o_ref):
    e = pl.program_id(1)

    @pl.when(e == 0)
    def _():
        o_ref[...] = x_ref[...]

    xn = xn_ref[...]
    h = jax.nn.silu(_dot(xn, wg_ref[...])) * _dot(xn, wu_ref[...])
    rt = rt_ref[...]
    lane = lax.broadcasted_iota(jnp.int32, rt.shape, 1)
    gate = jnp.sum(jnp.where(lane == N_GROUPS + e, rt, 0.0), axis=1, keepdims=True)
    o_ref[...] += gate * _dot(h.astype(BF16), wd_ref[...])


def _moe_dense(x, xn, rt, wg, wu, wd, tm):
    n, d = x.shape
    ne, _, ff = wg.shape
    row = lambda i, e: (i, 0)
    return pl.pallas_call(
        _moe_dense_kernel,
        grid=(n // tm, ne),
        in_specs=[pl.BlockSpec((tm, d), row), pl.BlockSpec((tm, d), row), pl.BlockSpec((tm, ROUTE_W), row),
                  pl.BlockSpec((None, d, ff), lambda i, e: (e, 0, 0)),
                  pl.BlockSpec((None, d, ff), lambda i, e: (e, 0, 0)),
                  pl.BlockSpec((None, ff, d), lambda i, e: (e, 0, 0))],
        out_specs=pl.BlockSpec((tm, d), row),
        out_shape=jax.ShapeDtypeStruct((n, d), F32),
        compiler_params=pltpu.CompilerParams(dimension_semantics=("arbitrary", "arbitrary")),
        name="moe_dense",
    )(x, xn, rt, wg, wu, wd)


def _final_norm_kernel(x_ref, g_ref, o_ref):
    o_ref[...] = _rms(x_ref[...]) * g_ref[...]


def _final_norm(x, g, tm):
    n, d = x.shape
    return pl.pallas_call(
        _final_norm_kernel,
        grid=(n // tm,),
        in_specs=[pl.BlockSpec((tm, d), lambda i: (i, 0)), pl.BlockSpec((1, d), lambda i: (0, 0))],
        out_specs=pl.BlockSpec((tm, d), lambda i: (i, 0)),
        out_shape=jax.ShapeDtypeStruct((n, d), F32),
        compiler_params=pltpu.CompilerParams(dimension_semantics=("arbitrary",)),
        name="final_norm",
    )(x, g.reshape(1, d))


def _perm_w_in(w_in, dims, m_heads, f_heads, prompt, dtype):
    lw, mw, fw = dims
    o_g = 2 * lw + 4 * mw
    o_q = o_g + 2 * m_heads
    o_k = o_q + fw
    o_v = o_k + fw
    o_ff = o_v + fw
    n_gate = 2 * m_heads + f_heads
    gates = jnp.concatenate([w_in[:, o_g:o_q], w_in[:, o_ff:o_ff + f_heads]], axis=1)
    gates = jnp.pad(gates, ((0, 0), (0, GATE_W - n_gate)))
    if prompt:
        w = jnp.concatenate([w_in[:, :o_g], w_in[:, o_k:o_v], gates], axis=1)
        return w.astype(dtype), w_in[:, o_q:o_ff].T.astype(dtype)
    w = jnp.concatenate([w_in[:, :o_g], w_in[:, o_q:o_k], gates, w_in[:, o_k:o_ff]], axis=1)
    return w.astype(dtype), None


def _route_weights(rg, rgb, re, reb):
    n = rg.shape[1] + re.shape[1]
    w = jnp.pad(jnp.concatenate([rg, re], axis=1), ((0, 0), (0, ROUTE_W - n)))
    b = jnp.pad(jnp.concatenate([rgb, reb]), (0, ROUTE_W - n)).reshape(1, ROUTE_W)
    return w, b


def _lru_gate_weights(wr, wi, dtype):
    nb, bs, _ = wr.shape
    eye = jnp.eye(nb, dtype=wr.dtype)

    def dense(w):
        return jnp.einsum("ncd,nm->ncmd", w, eye).reshape(nb * bs, nb * bs)

    return jnp.concatenate([dense(wr), dense(wi)], axis=1).astype(dtype)


def _tile(n, preferred):
    return preferred if n % preferred == 0 else n


def kernel(x_prompt, x_sample, cache_k, cache_v, cache_logf, page_table, state_conv, state_lru_h, state_mlstm_C, state_mlstm_n, state_mlstm_m, norm1, w_in, conv_w, conv_b, lru_wr, lru_br, lru_wi, lru_bi, lru_lambda, m_bi, m_bf, fox_bf, out_norm, w_out, norm2, router_g, router_g_b, router_e, router_e_b, w_gate, w_up, w_down, final_norm):
    depth, d = norm1.shape
    lw = conv_w.shape[-1]
    mh = m_bi.shape[-1]
    fh = fox_bf.shape[-1]
    dh = HEAD_DIM
    mw, fw = mh * dh, fh * dh
    dims = (lw, mw, fw)
    n_gate = 2 * mh + fh
    pool, page = cache_k.shape[1], cache_k.shape[2]
    ck = cache_k.transpose(0, 1, 3, 4, 2).reshape(depth, pool, fw, page)
    cv = cache_v.transpose(0, 1, 3, 4, 2).reshape(depth, pool, fw, page)
    clf = cache_logf.transpose(0, 1, 3, 2)

    def layer(x, l, conv_buf, h0, c0, n0, m0, prompt):
        b, t, _ = x.shape
        n = b * t
        x2 = x.reshape(n, d)
        hp = (not prompt) and l < depth - 1
        wdt = F32 if hp else BF16
        w_row, w_t = _perm_w_in(w_in[l], dims, mh, fh, prompt, wdt)
        if prompt:
            row_outs = [(2 * lw, F32, 1.0), (4 * mw, F32, 1.0), (fw, BF16, 1.0), (GATE_W, F32, 1.0)]
            t_outs = [(fw, (BF16,), LOG2E * dh ** -0.5), (fw, (F32,), 1.0), (fw, (F32, BF16), 1.0)]
            lru, mz, kb, gate, qt, kt, vt, vtb = _in_proj(x2, norm1[l], w_row, w_t, row_outs, t_outs,
                                                          _tile(n, 512), t, hp)
        else:
            row_outs = [(2 * lw, F32, 1.0), (4 * mw, F32, 1.0), (fw, wdt, dh ** -0.5), (GATE_W, F32, 1.0),
                        (fw, F32, 1.0), (fw, F32, 1.0)]
            lru, mz, fq, gate, k_new, v_new = _in_proj(x2, norm1[l], w_row, w_t, row_outs, [], _tile(n, 512), t, hp)

        tp = -(-t // CHUNK) * CHUNK
        gate3 = gate.reshape(b, t, GATE_W)
        if tp != t:
            gate3 = jnp.pad(gate3, ((0, 0), (0, tp - t), (0, 0)))
        gbias = jnp.pad(jnp.concatenate([m_bi[l], m_bf[l], fox_bf[l]]), (0, GATE_W - n_gate)).reshape(1, GATE_W)
        act, cum, act_t, cum_t = _gates(gate3, gbias, mh, 2 * mh, 2 * SUBLANES, _tile(tp, 4 * CHUNK))

        cbuf8 = jnp.pad(conv_buf, ((0, 0), (SUBLANES - (CONV_W - 1), 0), (0, 0)))
        wg = _lru_gate_weights(lru_wr[l], lru_wi[l], wdt)
        bg = jnp.concatenate([lru_br[l], lru_bi[l]]).reshape(1, 2 * lw)
        y_l, cn8, hn = _lru(lru.reshape(b, t, 2 * lw), cbuf8, h0.reshape(b, 1, lw), conv_w[l],
                            conv_b[l].reshape(1, lw), wg, bg, lru_lambda[l].reshape(1, lw), _tile(t, 256), hp)

        y_m, c_new, n_new, m_new = _mlstm(mz.reshape(b, t, 4 * mw), act, cum, act_t, cum_t, c0,
                                          n0.reshape(b, mh, 1, dh), m0.reshape(b, mh, 1, 1), mh, hp)

        if prompt:
            tq = _tile(t, 256)
            y_f = _fox_prompt(qt, kb.reshape(b, t, fw), vtb, cum, 2 * mh, tq, tq)
            k_state = kt.reshape(b, fh, dh, t).transpose(0, 3, 1, 2)
            v_state = vt.reshape(b, fh, dh, t).transpose(0, 3, 1, 2)
        else:
            g_t = cum_t[:, 2 * mh:2 * mh + fh, :t]
            y_f = _fox_decode(page_table, fq.reshape(b, t, fw), k_new.reshape(b, t, fw), v_new.reshape(b, t, fw),
                              g_t, ck, cv, clf, l, 16, hp)
            k_state = k_new.reshape(b, t, fh, dh)
            v_state = v_new.reshape(b, t, fh, dh)

        w_route, b_route = _route_weights(router_g[l], router_g_b[l], router_e[l], router_e_b[l])
        x1, xn, rt = _out_proj(x2, y_l.reshape(n, lw), y_m.reshape(n, mw), y_f.reshape(n, fw),
                               out_norm[l].reshape(1, d), w_out[l].astype(wdt), norm2[l].reshape(1, d),
                               w_route, b_route, _tile(n, 512), hp)
        x_next = _moe_dense(x1, xn, rt, w_gate[l].astype(BF16), w_up[l].astype(BF16), w_down[l].astype(BF16),
                            _tile(n, 1024))
        lf_state = act_t[:, 2 * mh:2 * mh + fh, :t].transpose(0, 2, 1)
        state = (k_state, v_state, lf_state, cn8[:, SUBLANES - (CONV_W - 1):], hn[:, 0], c_new,
                 n_new.reshape(b, mh, dh), m_new.reshape(b, mh))
        return x_next.reshape(b, t, d), state

    bp = x_prompt.shape[0]
    zc = jnp.zeros((bp, CONV_W - 1, lw), F32)
    zh = jnp.zeros((bp, lw), F32)
    z_c = jnp.zeros((bp, mh, dh, dh), F32)
    zn = jnp.zeros((bp, mh, dh), F32)
    zm = jnp.zeros((bp, mh), F32)

    xp, xs = x_prompt, x_sample
    st_p, st_s = [], []
    for l in range(depth):
        xp, sp = layer(xp, l, zc, zh, z_c, zn, zm, True)
        xs, ss = layer(xs, l, state_conv[l], state_lru_h[l], state_mlstm_C[l], state_mlstm_n[l],
                       state_mlstm_m[l], False)
        st_p.append(sp)
        st_s.append(ss)

    def fin(x):
        b, t, _ = x.shape
        return _final_norm(x.reshape(b * t, d), final_norm, _tile(b * t, 512)).reshape(b, t, d)

    def stk(states, i):
        return jnp.stack([s[i] for s in states])

    return ((fin(xp), fin(xs)) + tuple(stk(st_p, i) for i in range(8)) + tuple(stk(st_s, i) for i in range(8)))
```

```python
import functools
import math

import jax
import jax.numpy as jnp
from jax import lax
from jax.experimental import pallas as pl
from jax.experimental.pallas import tpu as pltpu

F32 = jnp.float32
BF16 = jnp.bfloat16

HEAD_DIM = 64
LRU_C = 8.0
CONV_W = 4
CHUNK = 128
N_GROUPS = 4
EXPERTS_PER_GROUP = 4
EPS = 1e-6
NEG = -1e30
LANES = 128
SUBLANES = 8
GATE_W = LANES

LOG2E = 1.4426950408889634


def _dot(a, b):
    return jnp.dot(a, b, preferred_element_type=F32)


def _dot_nt(a, b):
    return lax.dot_general(a, b, (((1,), (1,)), ((), ())), preferred_element_type=F32)


def _split(a):
    hi = a.astype(BF16)
    return hi, (a - hi.astype(F32)).astype(BF16)


def _mm(a, b, hp, nt=False):
    dot = _dot_nt if nt else _dot
    if not hp:
        return dot(a.astype(BF16), b.astype(BF16))
    ah, al = _split(a)
    bh, bl = _split(b)
    return dot(ah, bh) + (dot(al, bh) + dot(ah, bl))


def _mm_sel(a, sel, sel_left=False):
    hi = a.astype(BF16)
    r1 = a - hi.astype(F32)
    mid = r1.astype(BF16)
    lo = (r1 - mid.astype(F32)).astype(BF16)
    sel = sel.astype(BF16)
    if sel_left:
        return _dot(sel, hi) + (_dot(sel, mid) + _dot(sel, lo))
    return _dot(hi, sel) + (_dot(mid, sel) + _dot(lo, sel))


def _log_sigmoid(x):
    return jnp.minimum(x, 0.0) - jnp.log1p(jnp.exp(-jnp.abs(x)))


def _softplus(x):
    return jnp.maximum(x, 0.0) + jnp.log1p(jnp.exp(-jnp.abs(x)))


def _gelu_tanh(x):
    return 0.5 * x * (1.0 + jnp.tanh(math.sqrt(2.0 / math.pi) * (x + 0.044715 * (x * x * x))))


def _rms(x):
    return x * lax.rsqrt(jnp.mean(x * x, axis=-1, keepdims=True) + EPS)


def _in_proj_kernel(x_ref, g_ref, w_ref, *rest, row_spec, t_spec, hp):
    if t_spec:
        wt_ref, outs = rest[0], rest[1:]
    else:
        wt_ref, outs = None, rest
    xn = _rms(x_ref[...]) * g_ref[...]
    if not hp:
        xn = xn.astype(BF16)
    k = 0
    off = 0
    for width, scale in row_spec:
        r = _mm(xn, w_ref[:, off:off + width], hp)
        off += width
        if scale != 1.0:
            r = r * scale
        outs[k][...] = r.astype(outs[k].dtype)
        k += 1
    off = 0
    for rows, scale, copies in t_spec:
        r = _mm(wt_ref[off:off + rows, :], xn, hp, nt=True)
        off += rows
        if scale != 1.0:
            r = r * scale
        for _ in range(copies):
            outs[k][...] = r.astype(outs[k].dtype)
            k += 1


def _in_proj(x, n, g, w, w_t, row_outs, t_outs, tm, seq, hp):
    d = x.shape[1]
    row = lambda i: (i, 0)
    fixed = lambda i: (0, 0)
    per_b = seq // tm
    tmap = lambda i: (i // per_b, 0, i % per_b)
    out_specs = [pl.BlockSpec((tm, wd), row) for wd, _, _ in row_outs]
    out_shape = [jax.ShapeDtypeStruct((n, wd), dt) for wd, dt, _ in row_outs]
    ins = [x, g.reshape(1, d), w]
    in_specs = [pl.BlockSpec((tm, d), row), pl.BlockSpec((1, d), fixed), pl.BlockSpec(w.shape, fixed)]
    if t_outs:
        ins.append(w_t)
        in_specs.append(pl.BlockSpec(w_t.shape, fixed))
    for rows, dts, _ in t_outs:
        out_specs += [pl.BlockSpec((None, rows, tm), tmap)] * len(dts)
        out_shape += [jax.ShapeDtypeStruct((n // seq, rows, seq), dt) for dt in dts]
    return pl.pallas_call(
        functools.partial(_in_proj_kernel, row_spec=tuple((wd, sc) for wd, _, sc in row_outs),
                          t_spec=tuple((rows, sc, len(dts)) for rows, dts, sc in t_outs), hp=hp),
        grid=(n // tm,),
        in_specs=in_specs,
        out_specs=out_specs,
        out_shape=out_shape,
        compiler_params=pltpu.CompilerParams(dimension_semantics=("arbitrary",)),
        name="in_proj",
    )(*ins)


def _gates_kernel(z_ref, b_ref, act_ref, cum_ref, actT_ref, cumT_ref, carry_ref, *, n_in, n_local):
    c = pl.program_id(1)
    L = CHUNK
    rows = actT_ref.shape[0]

    @pl.when(c == 0)
    def _():
        carry_ref[...] = jnp.zeros_like(carry_ref)

    r = lax.broadcasted_iota(jnp.int32, (L, L), 0)
    s = lax.broadcasted_iota(jnp.int32, (L, L), 1)
    tri = jnp.where(s <= r, 1.0, 0.0)
    carry = carry_ref[0:1, :]
    for u in range(z_ref.shape[0] // L):
        lo = u * L
        pre = z_ref[lo:lo + L, :] + b_ref[...]
        lane = lax.broadcasted_iota(jnp.int32, pre.shape, 1)
        act = jnp.where(lane < n_in, pre, _log_sigmoid(pre))
        cum = _mm_sel(act, tri, sel_left=True) + jnp.where(lane[0:1, :] >= n_local, carry, 0.0)
        carry = cum[L - 1:L, :]
        act_ref[lo:lo + L, :] = act
        cum_ref[lo:lo + L, :] = cum
        actT_ref[:, lo:lo + L] = act.T[0:rows, :]
        cumT_ref[:, lo:lo + L] = cum.T[0:rows, :]
    carry_ref[...] = jnp.broadcast_to(carry, carry_ref.shape)


def _gates(z, bias, n_in, n_local, rows, tt):
    b, t, gw = z.shape
    blk = pl.BlockSpec((None, tt, gw), lambda i, c: (i, c, 0))
    blk_t = pl.BlockSpec((None, rows, tt), lambda i, c: (i, 0, c))
    return pl.pallas_call(
        functools.partial(_gates_kernel, n_in=n_in, n_local=n_local),
        grid=(b, t // tt),
        in_specs=[blk, pl.BlockSpec((1, gw), lambda i, c: (0, 0))],
        out_specs=[blk, blk, blk_t, blk_t],
        out_shape=[jax.ShapeDtypeStruct((b, t, gw), F32)] * 2 + [jax.ShapeDtypeStruct((b, rows, t), F32)] * 2,
        scratch_shapes=[pltpu.VMEM((SUBLANES, gw), F32)],
        compiler_params=pltpu.CompilerParams(dimension_semantics=("arbitrary", "arbitrary")),
        name="gates",
    )(z, bias)


def _lru_kernel(u_ref, cbuf_ref, h0_ref, cw_ref, cb_ref, wg_ref, bg_ref, lam_ref,
                y_ref, cnew_ref, hnew_ref, ext_ref, hc_ref, *, hp):
    t = pl.program_id(1)
    nt = pl.num_programs(1)
    tt = y_ref.shape[0]
    c = y_ref.shape[1]
    pad = SUBLANES

    @pl.when(t == 0)
    def _():
        ext_ref[0:pad, :] = cbuf_ref[...]
        hc_ref[...] = jnp.broadcast_to(h0_ref[...], hc_ref.shape)

    u = u_ref[:, 0:c]
    gl = u_ref[:, c:2 * c]
    ext_ref[pad:pad + tt, :] = u
    ext = ext_ref[...]
    xc = cw_ref[CONV_W - 1:CONV_W, :] * u + cb_ref[...]
    for j in range(CONV_W - 1):
        xc = xc + cw_ref[j:j + 1, :] * pltpu.roll(ext, CONV_W - 1 - j, 0)[pad:pad + tt, :]
    ext_ref[0:pad, :] = ext_ref[tt:tt + pad, :]

    pre = _mm(xc, wg_ref[...], hp) + bg_ref[...]
    r = jax.nn.sigmoid(pre[:, 0:c])
    i = jax.nn.sigmoid(pre[:, c:2 * c])
    log_a = (-LRU_C) * r * _softplus(-lam_ref[...])
    a = jnp.exp(log_a)
    th = jnp.tanh(log_a)
    bx = jnp.sqrt(-2.0 * th / (1.0 - th)) * (i * xc)

    row = lax.broadcasted_iota(jnp.int32, (tt, c), 0) & (SUBLANES - 1)
    s = 1
    while s < SUBLANES:
        keep = row >= s
        bx = jnp.where(keep, a * pltpu.roll(bx, s, 0) + bx, bx)
        a = jnp.where(keep, a * pltpu.roll(a, s, 0), a)
        s *= 2
    carry = hc_ref[...]
    gate = _gelu_tanh(gl)
    for g in range(tt // SUBLANES):
        lo = g * SUBLANES
        h = a[lo:lo + SUBLANES, :] * carry + bx[lo:lo + SUBLANES, :]
        y_ref[lo:lo + SUBLANES, :] = h * gate[lo:lo + SUBLANES, :]
        carry = jnp.broadcast_to(h[SUBLANES - 1:SUBLANES, :], (SUBLANES, c))
    hc_ref[...] = carry

    @pl.when(t == nt - 1)
    def _():
        cnew_ref[...] = ext_ref[0:pad, :]
        hnew_ref[...] = carry[0:1, :]


def _lru(u, cbuf8, h0, cw, cb, wg, bg, lam, tt, hp):
    b, t, c2 = u.shape
    c = c2 // 2
    fixed = lambda i, j: (0, 0)
    per_b = lambda i, j: (i, 0, 0)
    return pl.pallas_call(
        functools.partial(_lru_kernel, hp=hp),
        grid=(b, t // tt),
        in_specs=[
            pl.BlockSpec((None, tt, c2), lambda i, j: (i, j, 0)),
            pl.BlockSpec((None, SUBLANES, c), per_b),
            pl.BlockSpec((None, 1, c), per_b),
            pl.BlockSpec((CONV_W, c), fixed),
            pl.BlockSpec((1, c), fixed),
            pl.BlockSpec((c, c2), fixed),
            pl.BlockSpec((1, c2), fixed),
            pl.BlockSpec((1, c), fixed),
        ],
        out_specs=[
            pl.BlockSpec((None, tt, c), lambda i, j: (i, j, 0)),
            pl.BlockSpec((None, SUBLANES, c), per_b),
            pl.BlockSpec((None, 1, c), per_b),
        ],
        out_shape=[
            jax.ShapeDtypeStruct((b, t, c), F32),
            jax.ShapeDtypeStruct((b, SUBLANES, c), F32),
            jax.ShapeDtypeStruct((b, 1, c), F32),
        ],
        scratch_shapes=[pltpu.VMEM((tt + SUBLANES, c), F32), pltpu.VMEM((SUBLANES, c), F32)],
        compiler_params=pltpu.CompilerParams(dimension_semantics=("arbitrary", "arbitrary")),
        name="rg_lru",
    )(u, cbuf8, h0, cw, cb, wg, bg, lam)


def _mlstm_kernel(z_ref, act_ref, cum_ref, actT_ref, cumT_ref, c0_ref, n0_ref, m0_ref,
                  y_ref, c_ref, n_ref, m_ref, *, heads, tv, hp):
    ck = pl.program_id(1)
    L = act_ref.shape[0]
    dh = HEAD_DIM
    w = heads * dh

    @pl.when(ck == 0)
    def _():
        c_ref[...] = c0_ref[...]
        n_ref[...] = n0_ref[...]
        m_ref[...] = m0_ref[...]

    t_idx = lax.broadcasted_iota(jnp.int32, (L, L), 0)
    s_idx = lax.broadcasted_iota(jnp.int32, (L, L), 1)
    causal = s_idx <= t_idx
    col_ok = lax.broadcasted_iota(jnp.int32, (L, 1), 0) < tv
    row_ok = lax.broadcasted_iota(jnp.int32, (1, L), 1) < tv

    def rows(ref, lo):
        x = ref[:, lo:lo + dh]
        if tv < L:
            x = jnp.concatenate([x, jnp.zeros((L - tv, dh), F32)], axis=0)
        return x

    for h in range(heads):
        q = rows(z_ref, h * dh)
        k = rows(z_ref, w + h * dh) * (dh ** -0.5)
        v = rows(z_ref, 2 * w + h * dh)
        ic_col = act_ref[:, h:h + 1]
        bc_col = cum_ref[:, heads + h:heads + h + 1]
        ic_row = actT_ref[h:h + 1, :]
        bc_row = cumT_ref[heads + h:heads + h + 1, :]
        bc_last = bc_col[tv - 1:tv, :]
        if tv < L:
            ic_col = jnp.where(col_ok, ic_col, NEG)
            ic_row = jnp.where(row_ok, ic_row, NEG)
            bc_col = jnp.where(col_ok, bc_col, bc_last)
            bc_row = jnp.where(row_ok, bc_row, bc_last)
        m_prev = m_ref[h]
        c_prev = c_ref[h]
        n_prev = n_ref[h]

        dlog = jnp.where(causal, bc_col - bc_row + ic_row, NEG)
        inter = bc_col + m_prev
        mt = jnp.maximum(inter, jnp.max(dlog, axis=1, keepdims=True))
        qb = q if hp else q.astype(BF16)
        vb = v if hp else v.astype(BF16)
        sqk = _mm(qb, k, hp, nt=True) * jnp.exp(dlog - mt)
        decay = jnp.exp(inter - mt)
        num = _mm(sqk, vb, hp) + decay * _mm(qb, c_prev, hp)
        den = jnp.sum(sqk, axis=1, keepdims=True) + decay * jnp.sum(q * n_prev, axis=1, keepdims=True)
        hh = num / jnp.maximum(jnp.abs(den), jnp.exp(-mt))
        m_new = mt[L - 1:L, :]
        wk = jnp.exp(bc_last - bc_col + ic_col - m_new)
        g = jnp.exp(bc_last + m_prev - m_new)
        kw = k * wk
        c_ref[h] = g * c_prev + _mm(kw.T, vb, hp)
        n_ref[h] = g * n_prev + jnp.sum(kw, axis=0, keepdims=True)
        m_ref[h] = m_new
        o = z_ref[:, 3 * w + h * dh:3 * w + (h + 1) * dh]
        y_ref[:, h * dh:(h + 1) * dh] = jax.nn.sigmoid(o) * hh[0:tv, :]


def _mlstm(z, act, cum, actT, cumT, c0, n0, m0, heads, hp):
    b, t, w4 = z.shape
    w = w4 // 4
    L = CHUNK
    tv = min(t, L)
    nc = act.shape[1] // L
    gw = act.shape[-1]
    rows = actT.shape[1]
    dh = HEAD_DIM
    tok = lambda i, c: (i, c, 0)
    tokT = lambda i, c: (i, 0, c)
    st = lambda i, c: (i, 0, 0, 0)
    st_specs = [pl.BlockSpec((None, heads, dh, dh), st), pl.BlockSpec((None, heads, 1, dh), st),
                pl.BlockSpec((None, heads, 1, 1), st)]
    return pl.pallas_call(
        functools.partial(_mlstm_kernel, heads=heads, tv=tv, hp=hp),
        grid=(b, nc),
        in_specs=[pl.BlockSpec((None, tv, w4), tok), pl.BlockSpec((None, L, gw), tok), pl.BlockSpec((None, L, gw), tok),
                  pl.BlockSpec((None, rows, L), tokT), pl.BlockSpec((None, rows, L), tokT)] + st_specs,
        out_specs=[pl.BlockSpec((None, tv, w), tok)] + st_specs,
        out_shape=[jax.ShapeDtypeStruct((b, t, w), F32), jax.ShapeDtypeStruct(c0.shape, F32),
                   jax.ShapeDtypeStruct(n0.shape, F32), jax.ShapeDtypeStruct(m0.shape, F32)],
        compiler_params=pltpu.CompilerParams(dimension_semantics=("arbitrary", "arbitrary")),
        name="mlstm",
    )(z, act, cum, actT, cumT, c0, n0, m0)


def _fox_prompt_kernel(qt_ref, k_ref, vt_ref, cum_ref, y_ref, fb_ref, xs_ref, ml_ref, acc_ref, *, f_lane0, tk):
    i = pl.program_id(1)
    w, tq = qt_ref.shape
    dh = HEAD_DIM
    heads = w // dh
    sub = tq // tk
    reps = tq // LANES

    @pl.when(i == 0)
    def _():
        cum = cum_ref[...]
        lane = lax.broadcasted_iota(jnp.int32, cum.shape, 1)
        for h in range(heads):
            col = jnp.sum(jnp.where(lane == f_lane0 + h, cum, 0.0), axis=1, keepdims=True)
            fb_ref[h] = jnp.broadcast_to(col * LOG2E, fb_ref.shape[1:])

    row = lax.broadcasted_iota(jnp.int32, (LANES, tq), 0)
    qm = []
    for h in range(heads):
        qt = qt_ref[(h // 2) * LANES:(h // 2 + 1) * LANES, :]
        keep = (row < dh) if h % 2 == 0 else (row >= dh)
        qm.append(jnp.where(keep, qt, jnp.zeros_like(qt)))

    def scores(start, slot):
        for h in range(heads):
            kb = k_ref[pl.ds(start, tk), (h // 2) * LANES:(h // 2 + 1) * LANES]
            fb = fb_ref[h, pl.ds(start, tk), :]
            xs_ref[slot, h] = _dot(kb, qm[h]) - jnp.concatenate([fb] * reps, axis=1)

    def update(slot, start, diag_off):
        for h in range(heads):
            x = xs_ref[slot, h]
            if diag_off is not None:
                r = lax.broadcasted_iota(jnp.int32, x.shape, 0) + diag_off
                c = lax.broadcasted_iota(jnp.int32, x.shape, 1)
                x = jnp.where(r <= c, x, NEG)
            m = ml_ref[h, 0:1, :]
            m_new = jnp.maximum(m, jnp.max(x, axis=0, keepdims=True))
            alpha = jnp.exp2(m - m_new)
            pr = jnp.exp2(x - m_new)
            ml_ref[h, 0:1, :] = m_new
            ml_ref[h, 1:2, :] = alpha * ml_ref[h, 1:2, :] + jnp.sum(pr, axis=0, keepdims=True)
            vb = vt_ref[h * dh:(h + 1) * dh, pl.ds(start, tk)]
            acc_ref[h] = alpha * acc_ref[h] + _dot(vb, pr.astype(BF16))

    for h in range(heads):
        ml_ref[h, 0:1, :] = jnp.full((1, tq), NEG, F32)
        ml_ref[h, 1:2, :] = jnp.zeros((1, tq), F32)
    acc_ref[...] = jnp.zeros_like(acc_ref)
    n = i * sub
    scores(0, 0)

    @pl.loop(0, n)
    def _(j):
        scores(pl.multiple_of((j + 1) * tk, tk), (j + 1) % 2)
        update(j % 2, pl.multiple_of(j * tk, tk), None)

    for u in range(sub):
        if u + 1 < sub:
            scores(pl.multiple_of(i * tq + (u + 1) * tk, tk), (n + u + 1) % 2)
        update((n + u) % 2, pl.multiple_of(i * tq + u * tk, tk), u * tk)
    y_ref[...] = jnp.concatenate([acc_ref[h] / ml_ref[h, 1:2, :] for h in range(heads)], axis=0).T


def _fox_prompt(qt, k, vt, cum, f_lane0, tq, tk):
    b, w, s = qt.shape
    heads = w // HEAD_DIM
    return pl.pallas_call(
        functools.partial(_fox_prompt_kernel, f_lane0=f_lane0, tk=tk),
        grid=(b, s // tq),
        in_specs=[
            pl.BlockSpec((None, w, tq), lambda bi, i: (bi, 0, i)),
            pl.BlockSpec((None, s, w), lambda bi, i: (bi, 0, 0)),
            pl.BlockSpec((None, w, s), lambda bi, i: (bi, 0, 0)),
            pl.BlockSpec((None, s, cum.shape[-1]), lambda bi, i: (bi, 0, 0)),
        ],
        out_specs=pl.BlockSpec((None, tq, w), lambda bi, i: (bi, i, 0)),
        out_shape=jax.ShapeDtypeStruct((b, s, w), F32),
        scratch_shapes=[pltpu.VMEM((heads, s, LANES), F32), pltpu.VMEM((2, heads, tk, tq), F32),
                        pltpu.VMEM((heads, SUBLANES, tq), F32), pltpu.VMEM((heads, HEAD_DIM, tq), F32)],
        compiler_params=pltpu.CompilerParams(dimension_semantics=("arbitrary", "arbitrary")),
        name="fox_prompt",
    )(qt, k, vt, cum)


def _fox_decode_kernel(pt_ref, q_ref, kn_ref, vn_ref, gt_ref, *rest, pps, heads, hp):
    k_refs = rest[0:pps]
    v_refs = rest[pps:2 * pps]
    lf_refs = rest[2 * pps:3 * pps]
    y_ref = rest[3 * pps]
    qh_ref, ql_ref, m_ref, l_ref, acc_ref, car_ref = rest[3 * pps + 1:]
    j = pl.program_id(1)
    nj = pl.num_programs(1)
    t_new = q_ref.shape[0]
    w = q_ref.shape[1]
    dh = HEAD_DIM
    page = k_refs[0].shape[1]
    rows = heads * t_new
    lane_head = lax.broadcasted_iota(jnp.int32, (t_new, w), 1) // dh

    @pl.when(j == 0)
    def _():
        q = q_ref[...].astype(F32)
        qh, ql = _split(q)
        for h in range(heads):
            keep = lane_head == h
            qh_ref[h * t_new:(h + 1) * t_new, :] = jnp.where(keep, qh, jnp.zeros_like(qh))
            ql_ref[h * t_new:(h + 1) * t_new, :] = jnp.where(keep, ql, jnp.zeros_like(ql))
        m_ref[...] = jnp.full(m_ref.shape, NEG, F32)
        l_ref[...] = jnp.zeros_like(l_ref)
        acc_ref[...] = jnp.zeros_like(acc_ref)
        car_ref[...] = jnp.zeros_like(car_ref)

    qh = qh_ref[...]
    ql = ql_ref[...]

    def qk(k, nt=False):
        dot = _dot_nt if nt else _dot
        if not hp:
            return dot(qh, k.astype(BF16))
        kh, kl = _split(k)
        return dot(qh, kh) + (dot(ql, kh) + dot(qh, kl))

    def expand(r):
        return jnp.concatenate([jnp.broadcast_to(r[h:h + 1, :], (t_new, r.shape[1])) for h in range(heads)], axis=0)

    lft = jnp.concatenate([lf_refs[r][...] for r in range(pps)], axis=0)
    jj = lax.broadcasted_iota(jnp.int32, (page, 2 * page), 0)
    ss = lax.broadcasted_iota(jnp.int32, (page, 2 * page), 1)
    u_aug = jnp.where((jj > ss) | (ss >= page), 1.0, 0.0)
    suf = _mm_sel(lft, u_aug)
    carry = car_ref[...]
    xs = []
    for r in range(pps):
        r_loc = suf[r * heads:(r + 1) * heads, 0:page] + carry
        carry = carry + suf[r * heads:(r + 1) * heads, page:2 * page]
        xs.append(qk(k_refs[r][...]) + expand(r_loc))
    car_ref[...] = carry

    m_prev = m_ref[...]
    m_new = m_prev
    for x in xs:
        m_new = jnp.maximum(m_new, jnp.max(x, axis=1, keepdims=True))
    alpha = jnp.exp(m_prev - m_new)
    l = alpha * l_ref[...]
    acc = alpha * acc_ref[...]
    for r in range(pps):
        p = jnp.exp(xs[r] - m_new)
        l = l + jnp.sum(p, axis=1, keepdims=True)
        acc = acc + _mm(p, v_refs[r][...], hp, nt=True)
    m_ref[...] = m_new
    l_ref[...] = l
    acc_ref[...] = acc

    @pl.when(j == nj - 1)
    def _():
        x = qk(kn_ref[...], nt=True) - expand(gt_ref[...])
        tq = lax.broadcasted_iota(jnp.int32, (rows, t_new), 0) % t_new
        sk = lax.broadcasted_iota(jnp.int32, (rows, t_new), 1)
        x = jnp.where(sk <= tq, x, NEG)
        m_fin = jnp.maximum(m_new, jnp.max(x, axis=1, keepdims=True))
        a2 = jnp.exp(m_new - m_fin)
        p = jnp.exp(x - m_fin)
        l_fin = a2 * l + jnp.sum(p, axis=1, keepdims=True)
        o = (a2 * acc + _mm(p, vn_ref[...], hp)) / l_fin
        out = jnp.zeros((t_new, w), F32)
        for h in range(heads):
            out = jnp.where(lane_head == h, o[h * t_new:(h + 1) * t_new, :], out)
        y_ref[...] = out


def _fox_decode(page_table, q, k_new, v_new, g_t, cache_k, cache_v, cache_lf, layer, pps, hp):
    b, t_new, w = q.shape
    heads = cache_lf.shape[2]
    page = cache_k.shape[3]
    n_pages = page_table.shape[1]
    steps = n_pages // pps
    rows = heads * t_new

    def page_map(r):
        return lambda bi, j, pt: (layer, pt[bi, n_pages - 1 - (j * pps + r)], 0, 0)

    tok = lambda bi, j, pt: (bi, 0, 0)
    in_specs = [pl.BlockSpec((None, t_new, w), tok), pl.BlockSpec((None, t_new, w), tok),
                pl.BlockSpec((None, t_new, w), tok), pl.BlockSpec((None, heads, t_new), tok)]
    in_specs += [pl.BlockSpec((None, None, w, page), page_map(r)) for r in range(pps)]
    in_specs += [pl.BlockSpec((None, None, w, page), page_map(r)) for r in range(pps)]
    in_specs += [pl.BlockSpec((None, None, heads, page), page_map(r)) for r in range(pps)]
    grid_spec = pltpu.PrefetchScalarGridSpec(
        num_scalar_prefetch=1,
        grid=(b, steps),
        in_specs=in_specs,
        out_specs=pl.BlockSpec((None, t_new, w), tok),
        scratch_shapes=[pltpu.VMEM((rows, w), BF16), pltpu.VMEM((rows, w), BF16),
                        pltpu.VMEM((rows, 1), F32), pltpu.VMEM((rows, 1), F32),
                        pltpu.VMEM((rows, w), F32), pltpu.VMEM((heads, page), F32)],
    )
    return pl.pallas_call(
        functools.partial(_fox_decode_kernel, pps=pps, heads=heads, hp=hp),
        grid_spec=grid_spec,
        out_shape=jax.ShapeDtypeStruct((b, t_new, w), F32),
        compiler_params=pltpu.CompilerParams(dimension_semantics=("arbitrary", "arbitrary")),
        name="fox_decode",
    )(page_table, q, k_new, v_new, g_t, *([cache_k] * pps), *([cache_v] * pps), *([cache_lf] * pps))


ROUTE_W = LANES


def _route(logits):
    lane = lax.broadcasted_iota(jnp.int32, logits.shape, 1).astype(F32)
    big = float(ROUTE_W)

    def first_max(x):
        mx = jnp.max(x, axis=1, keepdims=True)
        return mx, jnp.min(jnp.where(x == mx, lane, big), axis=1, keepdims=True)

    gmask = lane < N_GROUPS
    lg = jnp.where(gmask, logits, NEG)
    mg, gidx = first_max(lg)
    pg_sel = 1.0 / jnp.sum(jnp.where(gmask, jnp.exp(lg - mg), 0.0), axis=1, keepdims=True)
    lo = N_GROUPS + gidx * EXPERTS_PER_GROUP
    le = jnp.where((lane >= lo) & (lane < lo + EXPERTS_PER_GROUP), logits, NEG)
    v1, i1 = first_max(le)
    le2 = jnp.where(lane == i1, NEG, le)
    v2, i2 = first_max(le2)
    e21 = jnp.exp(v2 - v1)
    w1 = pg_sel / (1.0 + e21)
    w2 = w1 * e21
    tile = jnp.where(lane == i1, w1, 0.0) + jnp.where(lane == i2, w2, 0.0)
    tile = jnp.where(lane == 0.0, i1 - N_GROUPS, tile)
    tile = jnp.where(lane == 1.0, i2 - N_GROUPS, tile)
    tile = jnp.where(lane == 2.0, w1, tile)
    tile = jnp.where(lane == 3.0, w2, tile)
    return tile


def _out_proj_kernel(x_ref, yl_ref, ym_ref, yf_ref, go_ref, wo_ref, g2_ref, wr_ref, br_ref,
                     xr_ref, *, hp):
    acc = x_ref[...]
    off = 0
    for y_ref in (yl_ref, ym_ref, yf_ref):
        wd = y_ref.shape[-1]
        yn = _rms(y_ref[...]) * go_ref[:, off:off + wd]
        acc = acc + _mm(yn, wo_ref[off:off + wd, :], hp)
        off += wd
    d = acc.shape[-1]
    xr_ref[:, 0:d] = acc
    xn = _rms(acc) * g2_ref[...]
    xr_ref[:, d:d + ROUTE_W] = _route(_mm(xn, wr_ref[...], True) + br_ref[...])


def _out_proj(x, n, yl, ym, yf, g_out, w_out, g2, w_route, b_route, tm, hp):
    d = x.shape[1]
    row = lambda i: (i, 0)
    fixed = lambda i: (0, 0)
    ins = (x, yl, ym, yf)
    return pl.pallas_call(
        functools.partial(_out_proj_kernel, hp=hp),
        grid=(n // tm,),
        in_specs=[pl.BlockSpec((tm, a.shape[1]), row) for a in ins] + [
            pl.BlockSpec((1, d), fixed), pl.BlockSpec(w_out.shape, fixed), pl.BlockSpec((1, d), fixed),
            pl.BlockSpec(w_route.shape, fixed), pl.BlockSpec((1, ROUTE_W), fixed)],
        out_specs=pl.BlockSpec((tm, d + ROUTE_W), row),
        out_shape=jax.ShapeDtypeStruct((n, d + ROUTE_W), F32),
        compiler_params=pltpu.CompilerParams(dimension_semantics=("arbitrary",)),
        name="out_proj",
    )(x, yl, ym, yf, g_out, w_out, g2, w_route, b_route)


MOE_ISSUE_UNROLL = 8


def _moe_kernel(tb_ref, nv_ref, tg_ref, tok_ref, xr_hbm, g2_ref, wg_ref, wu_ref, wd_ref, out_hbm,
                xbuf, obuf, gsem, ssem, *, tm, d, n):
    i = pl.program_id(0)
    nt = pl.num_programs(0)
    slot = i % 2

    def start_gather(tile, sl):
        def body(c, carry):
            for u in range(MOE_ISSUE_UNROLL):
                r = c * MOE_ISSUE_UNROLL + u
                t = tok_ref[tile * tm + r]
                pltpu.make_async_copy(xr_hbm.at[pl.ds(t, 1), :], xbuf.at[sl, pl.ds(r, 1), :], gsem.at[sl]).start()
            return carry
        lax.fori_loop(0, tm // MOE_ISSUE_UNROLL, body, 0)

    def wait_gather(sl):
        pltpu.make_async_copy(xr_hbm.at[pl.ds(0, tm), :], xbuf.at[sl], gsem.at[sl]).wait()

    def start_scatter(tile, sl):
        spare = n + tb_ref[tile] * tm

        def body(c, carry):
            for u in range(MOE_ISSUE_UNROLL):
                r = c * MOE_ISSUE_UNROLL + u
                t = jnp.where(r < nv_ref[tile], tok_ref[tile * tm + r], spare + r)
                pltpu.make_async_copy(obuf.at[sl, pl.ds(r, 1), :], out_hbm.at[pl.ds(t, 1), :], ssem.at[sl]).start()
            return carry
        lax.fori_loop(0, tm // MOE_ISSUE_UNROLL, body, 0)

    def wait_scatter(sl):
        pltpu.make_async_copy(obuf.at[sl], out_hbm.at[pl.ds(0, tm), :], ssem.at[sl]).wait()

    @pl.when(i == 0)
    def _():
        start_gather(0, 0)
        obuf[0] = jnp.zeros(obuf.shape[1:], F32)
        spare_blocks = (out_hbm.shape[0] - n) // tm
        fills = [pltpu.make_async_copy(obuf.at[0], out_hbm.at[pl.ds(n + blk * tm, tm), :], ssem.at[0])
                 for blk in range(spare_blocks)]
        for f in fills:
            f.start()
        for f in fills:
            f.wait()

    @pl.when(i + 1 < nt)
    def _():
        start_gather(i + 1, 1 - slot)

    wait_gather(slot)

    @pl.when(i >= 2)
    def _():
        wait_scatter(slot)

    x = xbuf[slot, :, 0:d]
    rt = xbuf[slot, :, d:d + ROUTE_W]
    xn = (_rms(x) * g2_ref[...]).astype(BF16)
    lane = lax.broadcasted_iota(jnp.int32, rt.shape, 1)
    first = N_GROUPS + EXPERTS_PER_GROUP * tg_ref[i]
    acc = x
    for j in range(EXPERTS_PER_GROUP):
        h = jax.nn.silu(_dot(xn, wg_ref[j])) * _dot(xn, wu_ref[j])
        gate = jnp.sum(jnp.where(lane == first + j, rt, 0.0), axis=1, keepdims=True)
        acc = acc + gate * _dot(h.astype(BF16), wd_ref[j])
    obuf[slot] = acc
    start_scatter(i, slot)

    @pl.when(i == nt - 1)
    def _():
        wait_scatter(slot)

        @pl.when(i >= 1)
        def _():
            wait_scatter(1 - slot)


def _moe(xr, n, g2, wg, wu, wd, tm):
    d = xr.shape[1] - ROUTE_W
    ne, _, ff = wg.shape
    epg = EXPERTS_PER_GROUP
    ng = ne // epg
    grp = (xr[:n, d] * (1.0 / epg)).astype(jnp.int32)
    onehot = (grp[:, None] == jnp.arange(ng, dtype=jnp.int32)[None, :]).astype(jnp.int32)
    csum = jnp.cumsum(onehot, axis=0)
    cnt = csum[-1]
    rank = jnp.sum((csum - onehot) * onehot, axis=1)
    tiles_g = (cnt + tm - 1) // tm
    tile_end = jnp.cumsum(tiles_g)
    tile_off = tile_end - tiles_g
    pos = (tile_off * tm)[grp] + rank
    n_tiles = n // tm + ng
    tok = jnp.zeros((n_tiles * tm,), jnp.int32).at[pos].set(jnp.arange(n, dtype=jnp.int32), unique_indices=True)
    tile = jnp.arange(n_tiles, dtype=jnp.int32)
    tg = jnp.minimum(jnp.sum((tile[:, None] >= tile_end[None, :]).astype(jnp.int32), axis=1), ng - 1)
    used = tile < tile_end[-1]
    nv = jnp.where(used, jnp.clip(cnt[tg] - (tile - tile_off[tg]) * tm, 0, tm), 0).astype(jnp.int32)
    tb = jnp.where(used, tg, ng + tile - tile_end[-1]).astype(jnp.int32)

    wmap = lambda i, tb_r, nv_r, tg_r, tok_r: (tg_r[i], 0, 0, 0)
    grid_spec = pltpu.PrefetchScalarGridSpec(
        num_scalar_prefetch=4,
        grid=(n_tiles,),
        in_specs=[pl.BlockSpec(memory_space=pl.ANY),
                  pl.BlockSpec((1, d), lambda i, *_: (0, 0)),
                  pl.BlockSpec((None, epg, d, ff), wmap),
                  pl.BlockSpec((None, epg, d, ff), wmap),
                  pl.BlockSpec((None, epg, ff, d), wmap)],
        out_specs=pl.BlockSpec(memory_space=pl.ANY),
        scratch_shapes=[pltpu.VMEM((2, tm, d + ROUTE_W), F32), pltpu.VMEM((2, tm, d), F32),
                        pltpu.SemaphoreType.DMA((2,)), pltpu.SemaphoreType.DMA((2,))],
    )
    return pl.pallas_call(
        functools.partial(_moe_kernel, tm=tm, d=d, n=n),
        grid_spec=grid_spec,
        out_shape=jax.ShapeDtypeStruct((n + 2 * ng * tm, d), F32),
        compiler_params=pltpu.CompilerParams(dimension_semantics=("arbitrary",)),
        name="moe",
    )(tb, nv, tg, tok, xr, g2, wg.reshape(ng, epg, d, ff), wu.reshape(ng, epg, d, ff), wd.reshape(ng, epg, ff, d))


def _final_norm_kernel(x_ref, g_ref, o_ref):
    o_ref[...] = _rms(x_ref[...]) * g_ref[...]


def _final_norm(x, n, g, tm):
    d = x.shape[1]
    return pl.pallas_call(
        _final_norm_kernel,
        grid=(n // tm,),
        in_specs=[pl.BlockSpec((tm, d), lambda i: (i, 0)), pl.BlockSpec((1, d), lambda i: (0, 0))],
        out_specs=pl.BlockSpec((tm, d), lambda i: (i, 0)),
        out_shape=jax.ShapeDtypeStruct((n, d), F32),
        compiler_params=pltpu.CompilerParams(dimension_semantics=("arbitrary",)),
        name="final_norm",
    )(x, g.reshape(1, d))


def _perm_w_in(w_in, dims, m_heads, f_heads, prompt, dtype):
    lw, mw, fw = dims
    o_g = 2 * lw + 4 * mw
    o_q = o_g + 2 * m_heads
    o_k = o_q + fw
    o_v = o_k + fw
    o_ff = o_v + fw
    n_gate = 2 * m_heads + f_heads
    gates = jnp.concatenate([w_in[:, o_g:o_q], w_in[:, o_ff:o_ff + f_heads]], axis=1)
    gates = jnp.pad(gates, ((0, 0), (0, GATE_W - n_gate)))
    if prompt:
        w = jnp.concatenate([w_in[:, :o_g], w_in[:, o_k:o_v], gates], axis=1)
        return w.astype(dtype), w_in[:, o_q:o_ff].T.astype(dtype)
    w = jnp.concatenate([w_in[:, :o_g], w_in[:, o_q:o_k], gates, w_in[:, o_k:o_ff]], axis=1)
    return w.astype(dtype), None


def _route_weights(rg, rgb, re, reb):
    n = rg.shape[1] + re.shape[1]
    w = jnp.pad(jnp.concatenate([rg, re], axis=1), ((0, 0), (0, ROUTE_W - n)))
    b = jnp.pad(jnp.concatenate([rgb, reb]), (0, ROUTE_W - n)).reshape(1, ROUTE_W)
    return w, b


def _lru_gate_weights(wr, wi, dtype):
    nb, bs, _ = wr.shape
    eye = jnp.eye(nb, dtype=wr.dtype)

    def dense(w):
        return jnp.einsum("ncd,nm->ncmd", w, eye).reshape(nb * bs, nb * bs)

    return jnp.concatenate([dense(wr), dense(wi)], axis=1).astype(dtype)


def _tile(n, preferred):
    return preferred if n % preferred == 0 else n


def kernel(x_prompt, x_sample, cache_k, cache_v, cache_logf, page_table, state_conv, state_lru_h, state_mlstm_C, state_mlstm_n, state_mlstm_m, norm1, w_in, conv_w, conv_b, lru_wr, lru_br, lru_wi, lru_bi, lru_lambda, m_bi, m_bf, fox_bf, out_norm, w_out, norm2, router_g, router_g_b, router_e, router_e_b, w_gate, w_up, w_down, final_norm):
    depth, d = norm1.shape
    lw = conv_w.shape[-1]
    mh = m_bi.shape[-1]
    fh = fox_bf.shape[-1]
    dh = HEAD_DIM
    mw, fw = mh * dh, fh * dh
    dims = (lw, mw, fw)
    n_gate = 2 * mh + fh
    pool, page = cache_k.shape[1], cache_k.shape[2]
    ck = cache_k.transpose(0, 1, 3, 4, 2).reshape(depth, pool, fw, page)
    cv = cache_v.transpose(0, 1, 3, 4, 2).reshape(depth, pool, fw, page)
    clf = cache_logf.transpose(0, 1, 3, 2)

    def layer(x2, b, t, l, conv_buf, h0, c0, n0, m0, prompt):
        n = b * t
        hp = (not prompt) and l < depth - 1
        wdt = F32 if hp else BF16
        w_row, w_t = _perm_w_in(w_in[l], dims, mh, fh, prompt, wdt)
        if prompt:
            row_outs = [(2 * lw, F32, 1.0), (4 * mw, F32, 1.0), (fw, BF16, 1.0), (GATE_W, F32, 1.0)]
            t_outs = [(fw, (BF16,), LOG2E * dh ** -0.5), (fw, (F32,), 1.0), (fw, (F32, BF16), 1.0)]
            lru, mz, kb, gate, qt, kt, vt, vtb = _in_proj(x2, n, norm1[l], w_row, w_t, row_outs, t_outs,
                                                          _tile(n, 512), t, hp)
        else:
            row_outs = [(2 * lw, F32, 1.0), (4 * mw, F32, 1.0), (fw, wdt, dh ** -0.5), (GATE_W, F32, 1.0),
                        (fw, F32, 1.0), (fw, F32, 1.0)]
            lru, mz, fq, gate, k_new, v_new = _in_proj(x2, n, norm1[l], w_row, w_t, row_outs, [], _tile(n, 512),
                                                       t, hp)

        tp = -(-t // CHUNK) * CHUNK
        gate3 = gate.reshape(b, t, GATE_W)
        if tp != t:
            gate3 = jnp.pad(gate3, ((0, 0), (0, tp - t), (0, 0)))
        gbias = jnp.pad(jnp.concatenate([m_bi[l], m_bf[l], fox_bf[l]]), (0, GATE_W - n_gate)).reshape(1, GATE_W)
        act, cum, act_t, cum_t = _gates(gate3, gbias, mh, 2 * mh, 2 * SUBLANES, _tile(tp, 4 * CHUNK))

        cbuf8 = jnp.pad(conv_buf, ((0, 0), (SUBLANES - (CONV_W - 1), 0), (0, 0)))
        wg = _lru_gate_weights(lru_wr[l], lru_wi[l], wdt)
        bg = jnp.concatenate([lru_br[l], lru_bi[l]]).reshape(1, 2 * lw)
        y_l, cn8, hn = _lru(lru.reshape(b, t, 2 * lw), cbuf8, h0.reshape(b, 1, lw), conv_w[l],
                            conv_b[l].reshape(1, lw), wg, bg, lru_lambda[l].reshape(1, lw), _tile(t, 256), hp)

        y_m, c_new, n_new, m_new = _mlstm(mz.reshape(b, t, 4 * mw), act, cum, act_t, cum_t, c0,
                                          n0.reshape(b, mh, 1, dh), m0.reshape(b, mh, 1, 1), mh, hp)

        if prompt:
            tq = _tile(t, 256)
            y_f = _fox_prompt(qt, kb.reshape(b, t, fw), vtb, cum, 2 * mh, tq, tq)
            k_state = kt.reshape(b, fh, dh, t).transpose(0, 3, 1, 2)
            v_state = vt.reshape(b, fh, dh, t).transpose(0, 3, 1, 2)
        else:
            g_t = cum_t[:, 2 * mh:2 * mh + fh, :t]
            y_f = _fox_decode(page_table, fq.reshape(b, t, fw), k_new.reshape(b, t, fw), v_new.reshape(b, t, fw),
                              g_t, ck, cv, clf, l, 16, hp)
            k_state = k_new.reshape(b, t, fh, dh)
            v_state = v_new.reshape(b, t, fh, dh)

        w_route, b_route = _route_weights(router_g[l], router_g_b[l], router_e[l], router_e_b[l])
        xr = _out_proj(x2, n, y_l.reshape(n, lw), y_m.reshape(n, mw), y_f.reshape(n, fw),
                       out_norm[l].reshape(1, d), w_out[l].astype(wdt), norm2[l].reshape(1, d),
                       w_route, b_route, _tile(n, 512), hp)
        x_next = _moe(xr, n, norm2[l].reshape(1, d), w_gate[l].astype(BF16), w_up[l].astype(BF16),
                      w_down[l].astype(BF16), 256 if n >= 4096 else 64)
        lf_state = act_t[:, 2 * mh:2 * mh + fh, :t].transpose(0, 2, 1)
        state = (k_state, v_state, lf_state, cn8[:, SUBLANES - (CONV_W - 1):], hn[:, 0], c_new,
                 n_new.reshape(b, mh, dh), m_new.reshape(b, mh))
        return x_next, state

    bp = x_prompt.shape[0]
    zc = jnp.zeros((bp, CONV_W - 1, lw), F32)
    zh = jnp.zeros((bp, lw), F32)
    z_c = jnp.zeros((bp, mh, dh, dh), F32)
    zn = jnp.zeros((bp, mh, dh), F32)
    zm = jnp.zeros((bp, mh), F32)

    bs, ts = x_sample.shape[0], x_sample.shape[1]
    tp = x_prompt.shape[1]
    xp, xs = x_prompt.reshape(bp * tp, d), x_sample.reshape(bs * ts, d)
    st_p, st_s = [], []
    for l in range(depth):
        xp, sp = layer(xp, bp, tp, l, zc, zh, z_c, zn, zm, True)
        xs, ss = layer(xs, bs, ts, l, state_conv[l], state_lru_h[l], state_mlstm_C[l], state_mlstm_n[l],
                       state_mlstm_m[l], False)
        st_p.append(sp)
        st_s.append(ss)

    def fin(x, b, t):
        return _final_norm(x, b * t, final_norm, _tile(b * t, 512)).reshape(b, t, d)

    def stk(states, i):
        return jnp.stack([s[i] for s in states])

    return ((fin(xp, bp, tp), fin(xs, bs, ts)) + tuple(stk(st_p, i) for i in range(8)) + tuple(stk(st_s, i) for i in range(8)))
```

```python
import functools
import math

import jax
import jax.numpy as jnp
from jax import lax
from jax.experimental import pallas as pl
from jax.experimental.pallas import tpu as pltpu

F32 = jnp.float32
BF16 = jnp.bfloat16

HEAD_DIM = 64
LRU_C = 8.0
CONV_W = 4
CHUNK = 128
N_GROUPS = 4
EXPERTS_PER_GROUP = 4
EPS = 1e-6
NEG = -1e30
LANES = 128
SUBLANES = 8
GATE_W = LANES

LOG2E = 1.4426950408889634


def _dot(a, b):
    return jnp.dot(a, b, preferred_element_type=F32)


def _dot_nt(a, b):
    return lax.dot_general(a, b, (((1,), (1,)), ((), ())), preferred_element_type=F32)


def _split(a):
    hi = a.astype(BF16)
    return hi, (a - hi.astype(F32)).astype(BF16)


def _mm(a, b, hp, nt=False):
    dot = _dot_nt if nt else _dot
    if not hp:
        return dot(a.astype(BF16), b.astype(BF16))
    ah, al = _split(a)
    bh, bl = _split(b)
    return dot(ah, bh) + (dot(al, bh) + dot(ah, bl))


def _mm_sel(a, sel, sel_left=False):
    hi = a.astype(BF16)
    r1 = a - hi.astype(F32)
    mid = r1.astype(BF16)
    lo = (r1 - mid.astype(F32)).astype(BF16)
    sel = sel.astype(BF16)
    if sel_left:
        return _dot(sel, hi) + (_dot(sel, mid) + _dot(sel, lo))
    return _dot(hi, sel) + (_dot(mid, sel) + _dot(lo, sel))


def _log_sigmoid(x):
    return jnp.minimum(x, 0.0) - jnp.log1p(jnp.exp(-jnp.abs(x)))


def _softplus(x):
    return jnp.maximum(x, 0.0) + jnp.log1p(jnp.exp(-jnp.abs(x)))


def _gelu_tanh(x):
    return 0.5 * x * (1.0 + jnp.tanh(math.sqrt(2.0 / math.pi) * (x + 0.044715 * (x * x * x))))


def _rms(x):
    return x * lax.rsqrt(jnp.mean(x * x, axis=-1, keepdims=True) + EPS)


def _in_proj_kernel(x_ref, g_ref, w_ref, *rest, row_spec, t_spec, hp):
    if t_spec:
        wt_ref, outs = rest[0], rest[1:]
    else:
        wt_ref, outs = None, rest
    xn = _rms(x_ref[...]) * g_ref[...]
    if not hp:
        xn = xn.astype(BF16)
    k = 0
    off = 0
    for width, scale in row_spec:
        r = _mm(xn, w_ref[:, off:off + width], hp)
        off += width
        if scale != 1.0:
            r = r * scale
        outs[k][...] = r.astype(outs[k].dtype)
        k += 1
    off = 0
    for rows, scale, copies in t_spec:
        r = _mm(wt_ref[off:off + rows, :], xn, hp, nt=True)
        off += rows
        if scale != 1.0:
            r = r * scale
        for _ in range(copies):
            outs[k][...] = r.astype(outs[k].dtype)
            k += 1


def _in_proj(x, n, g, w, w_t, row_outs, t_outs, tm, seq, hp):
    d = x.shape[1]
    row = lambda i: (i, 0)
    fixed = lambda i: (0, 0)
    per_b = seq // tm
    tmap = lambda i: (i // per_b, 0, i % per_b)
    out_specs = [pl.BlockSpec((tm, wd), row) for wd, _, _ in row_outs]
    out_shape = [jax.ShapeDtypeStruct((n, wd), dt) for wd, dt, _ in row_outs]
    ins = [x, g.reshape(1, d), w]
    in_specs = [pl.BlockSpec((tm, d), row), pl.BlockSpec((1, d), fixed), pl.BlockSpec(w.shape, fixed)]
    if t_outs:
        ins.append(w_t)
        in_specs.append(pl.BlockSpec(w_t.shape, fixed))
    for rows, dts, _ in t_outs:
        out_specs += [pl.BlockSpec((None, rows, tm), tmap)] * len(dts)
        out_shape += [jax.ShapeDtypeStruct((n // seq, rows, seq), dt) for dt in dts]
    return pl.pallas_call(
        functools.partial(_in_proj_kernel, row_spec=tuple((wd, sc) for wd, _, sc in row_outs),
                          t_spec=tuple((rows, sc, len(dts)) for rows, dts, sc in t_outs), hp=hp),
        grid=(n // tm,),
        in_specs=in_specs,
        out_specs=out_specs,
        out_shape=out_shape,
        compiler_params=pltpu.CompilerParams(dimension_semantics=("arbitrary",)),
        name="in_proj",
    )(*ins)


def _gates_kernel(z_ref, b_ref, act_ref, cum_ref, actT_ref, cumT_ref, carry_ref, *, n_in, n_local):
    c = pl.program_id(1)
    L = CHUNK
    rows = actT_ref.shape[0]

    @pl.when(c == 0)
    def _():
        carry_ref[...] = jnp.zeros_like(carry_ref)

    r = lax.broadcasted_iota(jnp.int32, (L, L), 0)
    s = lax.broadcasted_iota(jnp.int32, (L, L), 1)
    tri = jnp.where(s <= r, 1.0, 0.0)
    carry = carry_ref[0:1, :]
    for u in range(z_ref.shape[0] // L):
        lo = u * L
        pre = z_ref[lo:lo + L, :] + b_ref[...]
        lane = lax.broadcasted_iota(jnp.int32, pre.shape, 1)
        act = jnp.where(lane < n_in, pre, _log_sigmoid(pre))
        cum = _mm_sel(act, tri, sel_left=True) + jnp.where(lane[0:1, :] >= n_local, carry, 0.0)
        carry = cum[L - 1:L, :]
        act_ref[lo:lo + L, :] = act
        cum_ref[lo:lo + L, :] = cum
        actT_ref[:, lo:lo + L] = act.T[0:rows, :]
        cumT_ref[:, lo:lo + L] = cum.T[0:rows, :]
    carry_ref[...] = jnp.broadcast_to(carry, carry_ref.shape)


def _gates(z, bias, n_in, n_local, rows, tt):
    b, t, gw = z.shape
    blk = pl.BlockSpec((None, tt, gw), lambda i, c: (i, c, 0))
    blk_t = pl.BlockSpec((None, rows, tt), lambda i, c: (i, 0, c))
    return pl.pallas_call(
        functools.partial(_gates_kernel, n_in=n_in, n_local=n_local),
        grid=(b, t // tt),
        in_specs=[blk, pl.BlockSpec((1, gw), lambda i, c: (0, 0))],
        out_specs=[blk, blk, blk_t, blk_t],
        out_shape=[jax.ShapeDtypeStruct((b, t, gw), F32)] * 2 + [jax.ShapeDtypeStruct((b, rows, t), F32)] * 2,
        scratch_shapes=[pltpu.VMEM((SUBLANES, gw), F32)],
        compiler_params=pltpu.CompilerParams(dimension_semantics=("arbitrary", "arbitrary")),
        name="gates",
    )(z, bias)


def _lru_kernel(u_ref, cbuf_ref, h0_ref, cw_ref, cb_ref, wg_ref, bg_ref, lam_ref,
                y_ref, cnew_ref, hnew_ref, ext_ref, hc_ref, *, hp):
    t = pl.program_id(1)
    nt = pl.num_programs(1)
    tt = y_ref.shape[0]
    c = y_ref.shape[1]
    pad = SUBLANES

    @pl.when(t == 0)
    def _():
        ext_ref[0:pad, :] = cbuf_ref[...]
        hc_ref[...] = jnp.broadcast_to(h0_ref[...], hc_ref.shape)

    u = u_ref[:, 0:c]
    gl = u_ref[:, c:2 * c]
    ext_ref[pad:pad + tt, :] = u
    ext = ext_ref[...]
    xc = cw_ref[CONV_W - 1:CONV_W, :] * u + cb_ref[...]
    for j in range(CONV_W - 1):
        xc = xc + cw_ref[j:j + 1, :] * pltpu.roll(ext, CONV_W - 1 - j, 0)[pad:pad + tt, :]
    ext_ref[0:pad, :] = ext_ref[tt:tt + pad, :]

    pre = _mm(xc, wg_ref[...], hp) + bg_ref[...]
    r = jax.nn.sigmoid(pre[:, 0:c])
    i = jax.nn.sigmoid(pre[:, c:2 * c])
    log_a = (-LRU_C) * r * _softplus(-lam_ref[...])
    a = jnp.exp(log_a)
    th = jnp.tanh(log_a)
    bx = jnp.sqrt(-2.0 * th / (1.0 - th)) * (i * xc)

    row = lax.broadcasted_iota(jnp.int32, (tt, c), 0) & (SUBLANES - 1)
    s = 1
    while s < SUBLANES:
        keep = row >= s
        bx = jnp.where(keep, a * pltpu.roll(bx, s, 0) + bx, bx)
        a = jnp.where(keep, a * pltpu.roll(a, s, 0), a)
        s *= 2
    carry = hc_ref[...]
    gate = _gelu_tanh(gl)
    for g in range(tt // SUBLANES):
        lo = g * SUBLANES
        h = a[lo:lo + SUBLANES, :] * carry + bx[lo:lo + SUBLANES, :]
        y_ref[lo:lo + SUBLANES, :] = h * gate[lo:lo + SUBLANES, :]
        carry = jnp.broadcast_to(h[SUBLANES - 1:SUBLANES, :], (SUBLANES, c))
    hc_ref[...] = carry

    @pl.when(t == nt - 1)
    def _():
        cnew_ref[...] = ext_ref[0:pad, :]
        hnew_ref[...] = carry[0:1, :]


def _lru(u, cbuf8, h0, cw, cb, wg, bg, lam, tt, hp):
    b, t, c2 = u.shape
    c = c2 // 2
    fixed = lambda i, j: (0, 0)
    per_b = lambda i, j: (i, 0, 0)
    return pl.pallas_call(
        functools.partial(_lru_kernel, hp=hp),
        grid=(b, t // tt),
        in_specs=[
            pl.BlockSpec((None, tt, c2), lambda i, j: (i, j, 0)),
            pl.BlockSpec((None, SUBLANES, c), per_b),
            pl.BlockSpec((None, 1, c), per_b),
            pl.BlockSpec((CONV_W, c), fixed),
            pl.BlockSpec((1, c), fixed),
            pl.BlockSpec((c, c2), fixed),
            pl.BlockSpec((1, c2), fixed),
            pl.BlockSpec((1, c), fixed),
        ],
        out_specs=[
            pl.BlockSpec((None, tt, c), lambda i, j: (i, j, 0)),
            pl.BlockSpec((None, SUBLANES, c), per_b),
            pl.BlockSpec((None, 1, c), per_b),
        ],
        out_shape=[
            jax.ShapeDtypeStruct((b, t, c), F32),
            jax.ShapeDtypeStruct((b, SUBLANES, c), F32),
            jax.ShapeDtypeStruct((b, 1, c), F32),
        ],
        scratch_shapes=[pltpu.VMEM((tt + SUBLANES, c), F32), pltpu.VMEM((SUBLANES, c), F32)],
        compiler_params=pltpu.CompilerParams(dimension_semantics=("arbitrary", "arbitrary")),
        name="rg_lru",
    )(u, cbuf8, h0, cw, cb, wg, bg, lam)


def _mlstm_kernel(z_ref, act_ref, cum_ref, actT_ref, cumT_ref, c0_ref, n0_ref, m0_ref,
                  y_ref, c_ref, n_ref, m_ref, *, heads, tv, hp):
    ck = pl.program_id(1)
    L = act_ref.shape[0]
    dh = HEAD_DIM
    w = heads * dh

    @pl.when(ck == 0)
    def _():
        c_ref[...] = c0_ref[...]
        n_ref[...] = n0_ref[...]
        m_ref[...] = m0_ref[...]

    t_idx = lax.broadcasted_iota(jnp.int32, (L, L), 0)
    s_idx = lax.broadcasted_iota(jnp.int32, (L, L), 1)
    causal = s_idx <= t_idx
    col_ok = lax.broadcasted_iota(jnp.int32, (L, 1), 0) < tv
    row_ok = lax.broadcasted_iota(jnp.int32, (1, L), 1) < tv

    def rows(ref, lo):
        x = ref[:, lo:lo + dh]
        if tv < L:
            x = jnp.concatenate([x, jnp.zeros((L - tv, dh), F32)], axis=0)
        return x

    hs = range(heads)
    q, k, v, st = [], [], [], []
    for h in hs:
        qh = rows(z_ref, h * dh)
        q.append(qh)
        k.append(rows(z_ref, w + h * dh) * (dh ** -0.5))
        vh = rows(z_ref, 2 * w + h * dh)
        v.append(vh if hp else vh.astype(BF16))
    qb = [x if hp else x.astype(BF16) for x in q]
    qk = [_mm(qb[h], k[h], hp, nt=True) for h in hs]
    qc = [_mm(qb[h], c_ref[h], hp) for h in hs]
    for h in hs:
        ic_col = act_ref[:, h:h + 1]
        bc_col = cum_ref[:, heads + h:heads + h + 1]
        ic_row = actT_ref[h:h + 1, :]
        bc_row = cumT_ref[heads + h:heads + h + 1, :]
        bc_last = bc_col[tv - 1:tv, :]
        if tv < L:
            ic_col = jnp.where(col_ok, ic_col, NEG)
            ic_row = jnp.where(row_ok, ic_row, NEG)
            bc_col = jnp.where(col_ok, bc_col, bc_last)
            bc_row = jnp.where(row_ok, bc_row, bc_last)
        m_prev = m_ref[h]
        dlog = jnp.where(causal, bc_col - bc_row + ic_row, NEG)
        inter = bc_col + m_prev
        mt = jnp.maximum(inter, jnp.max(dlog, axis=1, keepdims=True))
        sqk = qk[h] * jnp.exp(dlog - mt)
        m_new = mt[L - 1:L, :]
        kw = k[h] * jnp.exp(bc_last - bc_col + ic_col - m_new)
        st.append((sqk, jnp.exp(inter - mt), mt, m_new, jnp.exp(bc_last + m_prev - m_new), kw))
    sv = [_mm(st[h][0], v[h], hp) for h in hs]
    kv = [_mm(st[h][5].T, v[h], hp) for h in hs]
    for h in hs:
        sqk, decay, mt, m_new, g, kw = st[h]
        n_prev = n_ref[h]
        num = sv[h] + decay * qc[h]
        den = jnp.sum(sqk, axis=1, keepdims=True) + decay * jnp.sum(q[h] * n_prev, axis=1, keepdims=True)
        hh = num / jnp.maximum(jnp.abs(den), jnp.exp(-mt))
        c_ref[h] = g * c_ref[h] + kv[h]
        n_ref[h] = g * n_prev + jnp.sum(kw, axis=0, keepdims=True)
        m_ref[h] = m_new
        o = z_ref[:, 3 * w + h * dh:3 * w + (h + 1) * dh]
        y_ref[:, h * dh:(h + 1) * dh] = jax.nn.sigmoid(o) * hh[0:tv, :]


def _mlstm(z, act, cum, actT, cumT, c0, n0, m0, heads, hp):
    b, t, w4 = z.shape
    w = w4 // 4
    L = CHUNK
    tv = min(t, L)
    nc = act.shape[1] // L
    gw = act.shape[-1]
    rows = actT.shape[1]
    dh = HEAD_DIM
    tok = lambda i, c: (i, c, 0)
    tokT = lambda i, c: (i, 0, c)
    st = lambda i, c: (i, 0, 0, 0)
    st_specs = [pl.BlockSpec((None, heads, dh, dh), st), pl.BlockSpec((None, heads, 1, dh), st),
                pl.BlockSpec((None, heads, 1, 1), st)]
    return pl.pallas_call(
        functools.partial(_mlstm_kernel, heads=heads, tv=tv, hp=hp),
        grid=(b, nc),
        in_specs=[pl.BlockSpec((None, tv, w4), tok), pl.BlockSpec((None, L, gw), tok), pl.BlockSpec((None, L, gw), tok),
                  pl.BlockSpec((None, rows, L), tokT), pl.BlockSpec((None, rows, L), tokT)] + st_specs,
        out_specs=[pl.BlockSpec((None, tv, w), tok)] + st_specs,
        out_shape=[jax.ShapeDtypeStruct((b, t, w), F32), jax.ShapeDtypeStruct(c0.shape, F32),
                   jax.ShapeDtypeStruct(n0.shape, F32), jax.ShapeDtypeStruct(m0.shape, F32)],
        compiler_params=pltpu.CompilerParams(dimension_semantics=("arbitrary", "arbitrary")),
        name="mlstm",
    )(z, act, cum, actT, cumT, c0, n0, m0)


def _fox_prompt_kernel(qt_ref, k_ref, vt_ref, cum_ref, y_ref, fb_ref, xs_ref, ml_ref, acc_ref, *, f_lane0, tk):
    i = pl.program_id(1)
    w, tq = qt_ref.shape
    dh = HEAD_DIM
    heads = w // dh
    sub = tq // tk
    reps = tq // LANES

    @pl.when(i == 0)
    def _():
        cum = cum_ref[...]
        lane = lax.broadcasted_iota(jnp.int32, cum.shape, 1)
        for h in range(heads):
            col = jnp.sum(jnp.where(lane == f_lane0 + h, cum, 0.0), axis=1, keepdims=True)
            fb_ref[h] = jnp.broadcast_to(col * LOG2E, fb_ref.shape[1:])

    row = lax.broadcasted_iota(jnp.int32, (LANES, tq), 0)
    qm = []
    for h in range(heads):
        qt = qt_ref[(h // 2) * LANES:(h // 2 + 1) * LANES, :]
        keep = (row < dh) if h % 2 == 0 else (row >= dh)
        qm.append(jnp.where(keep, qt, jnp.zeros_like(qt)))

    def scores(start, slot):
        for h in range(heads):
            kb = k_ref[pl.ds(start, tk), (h // 2) * LANES:(h // 2 + 1) * LANES]
            fb = fb_ref[h, pl.ds(start, tk), :]
            xs_ref[slot, h] = _dot(kb, qm[h]) - jnp.concatenate([fb] * reps, axis=1)

    def update(slot, start, diag_off):
        for h in range(heads):
            x = xs_ref[slot, h]
            if diag_off is not None:
                r = lax.broadcasted_iota(jnp.int32, x.shape, 0) + diag_off
                c = lax.broadcasted_iota(jnp.int32, x.shape, 1)
                x = jnp.where(r <= c, x, NEG)
            m = ml_ref[h, 0:1, :]
            m_new = jnp.maximum(m, jnp.max(x, axis=0, keepdims=True))
            alpha = jnp.exp2(m - m_new)
            pr = jnp.exp2(x - m_new)
            ml_ref[h, 0:1, :] = m_new
            ml_ref[h, 1:2, :] = alpha * ml_ref[h, 1:2, :] + jnp.sum(pr, axis=0, keepdims=True)
            vb = vt_ref[h * dh:(h + 1) * dh, pl.ds(start, tk)]
            acc_ref[h] = alpha * acc_ref[h] + _dot(vb, pr.astype(BF16))

    for h in range(heads):
        ml_ref[h, 0:1, :] = jnp.full((1, tq), NEG, F32)
        ml_ref[h, 1:2, :] = jnp.zeros((1, tq), F32)
    acc_ref[...] = jnp.zeros_like(acc_ref)
    n = i * sub
    scores(0, 0)

    @pl.loop(0, n)
    def _(j):
        scores(pl.multiple_of((j + 1) * tk, tk), (j + 1) % 2)
        update(j % 2, pl.multiple_of(j * tk, tk), None)

    for u in range(sub):
        if u + 1 < sub:
            scores(pl.multiple_of(i * tq + (u + 1) * tk, tk), (n + u + 1) % 2)
        update((n + u) % 2, pl.multiple_of(i * tq + u * tk, tk), u * tk)
    y_ref[...] = jnp.concatenate([acc_ref[h] / ml_ref[h, 1:2, :] for h in range(heads)], axis=0).T


def _fox_prompt(qt, k, vt, cum, f_lane0, tq, tk):
    b, w, s = qt.shape
    heads = w // HEAD_DIM
    return pl.pallas_call(
        functools.partial(_fox_prompt_kernel, f_lane0=f_lane0, tk=tk),
        grid=(b, s // tq),
        in_specs=[
            pl.BlockSpec((None, w, tq), lambda bi, i: (bi, 0, i)),
            pl.BlockSpec((None, s, w), lambda bi, i: (bi, 0, 0)),
            pl.BlockSpec((None, w, s), lambda bi, i: (bi, 0, 0)),
            pl.BlockSpec((None, s, cum.shape[-1]), lambda bi, i: (bi, 0, 0)),
        ],
        out_specs=pl.BlockSpec((None, tq, w), lambda bi, i: (bi, i, 0)),
        out_shape=jax.ShapeDtypeStruct((b, s, w), F32),
        scratch_shapes=[pltpu.VMEM((heads, s, LANES), F32), pltpu.VMEM((2, heads, tk, tq), F32),
                        pltpu.VMEM((heads, SUBLANES, tq), F32), pltpu.VMEM((heads, HEAD_DIM, tq), F32)],
        compiler_params=pltpu.CompilerParams(dimension_semantics=("arbitrary", "arbitrary")),
        name="fox_prompt",
    )(qt, k, vt, cum)


DECODE_GROUPS = 2


def _fox_decode_kernel(pt_ref, q_ref, kn_ref, vn_ref, gt_ref, *rest, pps, heads, hp):
    k_refs = rest[0:pps]
    v_refs = rest[pps:2 * pps]
    lf_refs = rest[2 * pps:3 * pps]
    y_ref = rest[3 * pps]
    qh_ref, ql_ref, m_ref, l_ref, acc_ref, car_ref = rest[3 * pps + 1:]
    j = pl.program_id(1)
    nj = pl.num_programs(1)
    t_new = q_ref.shape[0]
    w = q_ref.shape[1]
    dh = HEAD_DIM
    page = k_refs[0].shape[1]
    rows = heads * t_new
    lane_head = lax.broadcasted_iota(jnp.int32, (t_new, w), 1) // dh

    @pl.when(j == 0)
    def _():
        q = q_ref[...].astype(F32)
        qh, ql = _split(q)
        for h in range(heads):
            keep = lane_head == h
            qh_ref[h * t_new:(h + 1) * t_new, :] = jnp.where(keep, qh, jnp.zeros_like(qh))
            ql_ref[h * t_new:(h + 1) * t_new, :] = jnp.where(keep, ql, jnp.zeros_like(ql))
        m_ref[...] = jnp.full(m_ref.shape, NEG, F32)
        l_ref[...] = jnp.zeros_like(l_ref)
        acc_ref[...] = jnp.zeros_like(acc_ref)
        car_ref[...] = jnp.zeros_like(car_ref)

    qh = qh_ref[...]
    ql = ql_ref[...]
    qhl = jnp.concatenate([qh, ql], axis=0)

    def qk(k, nt=False):
        dot = _dot_nt if nt else _dot
        if not hp:
            return dot(qh, k.astype(BF16))
        kh, kl = _split(k)
        both = dot(qhl, kh)
        return both[0:rows, :] + (both[rows:2 * rows, :] + dot(qh, kl))

    def pv(p, v, nt):
        dot = _dot_nt if nt else _dot
        if not hp:
            return dot(p.astype(BF16), v.astype(BF16))
        ph, pl_ = _split(p)
        vh, vl = _split(v)
        both = dot(jnp.concatenate([ph, pl_], axis=0), vh)
        return both[0:rows, :] + (both[rows:2 * rows, :] + dot(ph, vl))

    def expand(r):
        return jnp.concatenate([jnp.broadcast_to(r[h:h + 1, :], (t_new, r.shape[1])) for h in range(heads)], axis=0)

    lft = jnp.concatenate([lf_refs[r][...] for r in range(pps)], axis=0)
    jj = lax.broadcasted_iota(jnp.int32, (page, 2 * page), 0)
    ss = lax.broadcasted_iota(jnp.int32, (page, 2 * page), 1)
    u_aug = jnp.where((jj > ss) | (ss >= page), 1.0, 0.0)
    suf = _mm_sel(lft, u_aug)
    carry = car_ref[...]
    xs = []
    for r in range(pps):
        r_loc = suf[r * heads:(r + 1) * heads, 0:page] + carry
        carry = carry + suf[r * heads:(r + 1) * heads, page:2 * page]
        xs.append(qk(k_refs[r][...]) + expand(r_loc))
    car_ref[...] = carry

    m_new = m_ref[...]
    l = l_ref[...]
    acc = acc_ref[...]
    per = pps // DECODE_GROUPS
    for gi in range(DECODE_GROUPS):
        grp = range(gi * per, (gi + 1) * per)
        m_prev = m_new
        for r in grp:
            m_new = jnp.maximum(m_new, jnp.max(xs[r], axis=1, keepdims=True))
        alpha = jnp.exp(m_prev - m_new)
        l = alpha * l
        acc = alpha * acc
        for r in grp:
            p = jnp.exp(xs[r] - m_new)
            l = l + jnp.sum(p, axis=1, keepdims=True)
            acc = acc + pv(p, v_refs[r][...], True)
    m_ref[...] = m_new
    l_ref[...] = l
    acc_ref[...] = acc

    @pl.when(j == nj - 1)
    def _():
        x = qk(kn_ref[...], nt=True) - expand(gt_ref[...])
        tq = lax.broadcasted_iota(jnp.int32, (rows, t_new), 0) % t_new
        sk = lax.broadcasted_iota(jnp.int32, (rows, t_new), 1)
        x = jnp.where(sk <= tq, x, NEG)
        m_fin = jnp.maximum(m_new, jnp.max(x, axis=1, keepdims=True))
        a2 = jnp.exp(m_new - m_fin)
        p = jnp.exp(x - m_fin)
        l_fin = a2 * l + jnp.sum(p, axis=1, keepdims=True)
        o = (a2 * acc + pv(p, vn_ref[...], False)) / l_fin
        out = jnp.zeros((t_new, w), F32)
        for h in range(heads):
            out = jnp.where(lane_head == h, o[h * t_new:(h + 1) * t_new, :], out)
        y_ref[...] = out


def _fox_decode(page_table, q, k_new, v_new, g_t, cache_k, cache_v, cache_lf, layer, pps, hp):
    b, t_new, w = q.shape
    heads = cache_lf.shape[2]
    page = cache_k.shape[3]
    n_pages = page_table.shape[1]
    steps = n_pages // pps
    rows = heads * t_new

    def page_map(r):
        return lambda bi, j, pt: (layer, pt[bi, n_pages - 1 - (j * pps + r)], 0, 0)

    tok = lambda bi, j, pt: (bi, 0, 0)
    in_specs = [pl.BlockSpec((None, t_new, w), tok), pl.BlockSpec((None, t_new, w), tok),
                pl.BlockSpec((None, t_new, w), tok), pl.BlockSpec((None, heads, t_new), tok)]
    in_specs += [pl.BlockSpec((None, None, w, page), page_map(r)) for r in range(pps)]
    in_specs += [pl.BlockSpec((None, None, w, page), page_map(r)) for r in range(pps)]
    in_specs += [pl.BlockSpec((None, None, heads, page), page_map(r)) for r in range(pps)]
    grid_spec = pltpu.PrefetchScalarGridSpec(
        num_scalar_prefetch=1,
        grid=(b, steps),
        in_specs=in_specs,
        out_specs=pl.BlockSpec((None, t_new, w), tok),
        scratch_shapes=[pltpu.VMEM((rows, w), BF16), pltpu.VMEM((rows, w), BF16),
                        pltpu.VMEM((rows, 1), F32), pltpu.VMEM((rows, 1), F32),
                        pltpu.VMEM((rows, w), F32), pltpu.VMEM((heads, page), F32)],
    )
    return pl.pallas_call(
        functools.partial(_fox_decode_kernel, pps=pps, heads=heads, hp=hp),
        grid_spec=grid_spec,
        out_shape=jax.ShapeDtypeStruct((b, t_new, w), F32),
        compiler_params=pltpu.CompilerParams(dimension_semantics=("arbitrary", "arbitrary")),
        name="fox_decode",
    )(page_table, q, k_new, v_new, g_t, *([cache_k] * pps), *([cache_v] * pps), *([cache_lf] * pps))


ROUTE_W = LANES


def _route(logits):
    lane = lax.broadcasted_iota(jnp.int32, logits.shape, 1).astype(F32)
    big = float(ROUTE_W)

    def first_max(x):
        mx = jnp.max(x, axis=1, keepdims=True)
        return mx, jnp.min(jnp.where(x == mx, lane, big), axis=1, keepdims=True)

    gmask = lane < N_GROUPS
    lg = jnp.where(gmask, logits, NEG)
    mg, gidx = first_max(lg)
    pg_sel = 1.0 / jnp.sum(jnp.where(gmask, jnp.exp(lg - mg), 0.0), axis=1, keepdims=True)
    lo = N_GROUPS + gidx * EXPERTS_PER_GROUP
    le = jnp.where((lane >= lo) & (lane < lo + EXPERTS_PER_GROUP), logits, NEG)
    v1, i1 = first_max(le)
    le2 = jnp.where(lane == i1, NEG, le)
    v2, i2 = first_max(le2)
    e21 = jnp.exp(v2 - v1)
    w1 = pg_sel / (1.0 + e21)
    w2 = w1 * e21
    tile = jnp.where(lane == i1, w1, 0.0) + jnp.where(lane == i2, w2, 0.0)
    tile = jnp.where(lane == 0.0, i1 - N_GROUPS, tile)
    tile = jnp.where(lane == 1.0, i2 - N_GROUPS, tile)
    tile = jnp.where(lane == 2.0, w1, tile)
    tile = jnp.where(lane == 3.0, w2, tile)
    return tile


def _out_proj_kernel(x_ref, yl_ref, ym_ref, yf_ref, go_ref, wo_ref, g2_ref, wr_ref, br_ref,
                     xr_ref, *, hp):
    acc = x_ref[...]
    off = 0
    for y_ref in (yl_ref, ym_ref, yf_ref):
        wd = y_ref.shape[-1]
        yn = _rms(y_ref[...]) * go_ref[:, off:off + wd]
        acc = acc + _mm(yn, wo_ref[off:off + wd, :], hp)
        off += wd
    d = acc.shape[-1]
    xr_ref[:, 0:d] = acc
    xn = _rms(acc) * g2_ref[...]
    xr_ref[:, d:d + ROUTE_W] = _route(_mm(xn, wr_ref[...], True) + br_ref[...])


def _out_proj(x, n, yl, ym, yf, g_out, w_out, g2, w_route, b_route, tm, hp):
    d = x.shape[1]
    row = lambda i: (i, 0)
    fixed = lambda i: (0, 0)
    ins = (x, yl, ym, yf)
    return pl.pallas_call(
        functools.partial(_out_proj_kernel, hp=hp),
        grid=(n // tm,),
        in_specs=[pl.BlockSpec((tm, a.shape[1]), row) for a in ins] + [
            pl.BlockSpec((1, d), fixed), pl.BlockSpec(w_out.shape, fixed), pl.BlockSpec((1, d), fixed),
            pl.BlockSpec(w_route.shape, fixed), pl.BlockSpec((1, ROUTE_W), fixed)],
        out_specs=pl.BlockSpec((tm, d + ROUTE_W), row),
        out_shape=jax.ShapeDtypeStruct((n, d + ROUTE_W), F32),
        compiler_params=pltpu.CompilerParams(dimension_semantics=("arbitrary",)),
        name="out_proj",
    )(x, yl, ym, yf, g_out, w_out, g2, w_route, b_route)


MOE_ISSUE_UNROLL = 8


def _moe_kernel(tb_ref, nv_ref, tg_ref, tok_ref, xr_hbm, g2_ref, wg_ref, wu_ref, wd_ref, out_hbm,
                xbuf, obuf, gsem, ssem, *, tm, d, n):
    i = pl.program_id(0)
    nt = pl.num_programs(0)
    slot = i % 2

    def start_gather(tile, sl):
        def body(c, carry):
            for u in range(MOE_ISSUE_UNROLL):
                r = c * MOE_ISSUE_UNROLL + u
                t = tok_ref[tile * tm + r]
                pltpu.make_async_copy(xr_hbm.at[pl.ds(t, 1), :], xbuf.at[sl, pl.ds(r, 1), :], gsem.at[sl]).start()
            return carry
        lax.fori_loop(0, tm // MOE_ISSUE_UNROLL, body, 0)

    def wait_gather(sl):
        pltpu.make_async_copy(xr_hbm.at[pl.ds(0, tm), :], xbuf.at[sl], gsem.at[sl]).wait()

    def start_scatter(tile, sl):
        spare = n + tb_ref[tile] * tm

        def body(c, carry):
            for u in range(MOE_ISSUE_UNROLL):
                r = c * MOE_ISSUE_UNROLL + u
                t = jnp.where(r < nv_ref[tile], tok_ref[tile * tm + r], spare + r)
                pltpu.make_async_copy(obuf.at[sl, pl.ds(r, 1), :], out_hbm.at[pl.ds(t, 1), :], ssem.at[sl]).start()
            return carry
        lax.fori_loop(0, tm // MOE_ISSUE_UNROLL, body, 0)

    def wait_scatter(sl):
        pltpu.make_async_copy(obuf.at[sl], out_hbm.at[pl.ds(0, tm), :], ssem.at[sl]).wait()

    @pl.when(i == 0)
    def _():
        start_gather(0, 0)
        obuf[0] = jnp.zeros(obuf.shape[1:], F32)
        spare_blocks = (out_hbm.shape[0] - n) // tm
        fills = [pltpu.make_async_copy(obuf.at[0], out_hbm.at[pl.ds(n + blk * tm, tm), :], ssem.at[0])
                 for blk in range(spare_blocks)]
        for f in fills:
            f.start()
        for f in fills:
            f.wait()

    @pl.when(i + 1 < nt)
    def _():
        start_gather(i + 1, 1 - slot)

    wait_gather(slot)

    @pl.when(i >= 2)
    def _():
        wait_scatter(slot)

    x = xbuf[slot, :, 0:d]
    rt = xbuf[slot, :, d:d + ROUTE_W]
    xn = (_rms(x) * g2_ref[...]).astype(BF16)
    lane = lax.broadcasted_iota(jnp.int32, rt.shape, 1)
    first = N_GROUPS + EXPERTS_PER_GROUP * tg_ref[i]
    acc = x
    for j in range(EXPERTS_PER_GROUP):
        h = jax.nn.silu(_dot(xn, wg_ref[j])) * _dot(xn, wu_ref[j])
        gate = jnp.sum(jnp.where(lane == first + j, rt, 0.0), axis=1, keepdims=True)
        acc = acc + gate * _dot(h.astype(BF16), wd_ref[j])
    obuf[slot] = acc
    start_scatter(i, slot)

    @pl.when(i == nt - 1)
    def _():
        wait_scatter(slot)

        @pl.when(i >= 1)
        def _():
            wait_scatter(1 - slot)


def _moe(xr, n, g2, wg, wu, wd, tm):
    d = xr.shape[1] - ROUTE_W
    ne, _, ff = wg.shape
    epg = EXPERTS_PER_GROUP
    ng = ne // epg
    grp = (xr[:n, d] * (1.0 / epg)).astype(jnp.int32)
    onehot = (grp[:, None] == jnp.arange(ng, dtype=jnp.int32)[None, :]).astype(jnp.int32)
    csum = jnp.cumsum(onehot, axis=0)
    cnt = csum[-1]
    rank = jnp.sum((csum - onehot) * onehot, axis=1)
    tiles_g = (cnt + tm - 1) // tm
    tile_end = jnp.cumsum(tiles_g)
    tile_off = tile_end - tiles_g
    pos = (tile_off * tm)[grp] + rank
    n_tiles = n // tm + ng
    tok = jnp.zeros((n_tiles * tm,), jnp.int32).at[pos].set(jnp.arange(n, dtype=jnp.int32), unique_indices=True)
    tile = jnp.arange(n_tiles, dtype=jnp.int32)
    tg = jnp.minimum(jnp.sum((tile[:, None] >= tile_end[None, :]).astype(jnp.int32), axis=1), ng - 1)
    used = tile < tile_end[-1]
    nv = jnp.where(used, jnp.clip(cnt[tg] - (tile - tile_off[tg]) * tm, 0, tm), 0).astype(jnp.int32)
    tb = jnp.where(used, tg, ng + tile - tile_end[-1]).astype(jnp.int32)

    wmap = lambda i, tb_r, nv_r, tg_r, tok_r: (tg_r[i], 0, 0, 0)
    grid_spec = pltpu.PrefetchScalarGridSpec(
        num_scalar_prefetch=4,
        grid=(n_tiles,),
        in_specs=[pl.BlockSpec(memory_space=pl.ANY),
                  pl.BlockSpec((1, d), lambda i, *_: (0, 0)),
                  pl.BlockSpec((None, epg, d, ff), wmap),
                  pl.BlockSpec((None, epg, d, ff), wmap),
                  pl.BlockSpec((None, epg, ff, d), wmap)],
        out_specs=pl.BlockSpec(memory_space=pl.ANY),
        scratch_shapes=[pltpu.VMEM((2, tm, d + ROUTE_W), F32), pltpu.VMEM((2, tm, d), F32),
                        pltpu.SemaphoreType.DMA((2,)), pltpu.SemaphoreType.DMA((2,))],
    )
    return pl.pallas_call(
        functools.partial(_moe_kernel, tm=tm, d=d, n=n),
        grid_spec=grid_spec,
        out_shape=jax.ShapeDtypeStruct((n + 2 * ng * tm, d), F32),
        compiler_params=pltpu.CompilerParams(dimension_semantics=("arbitrary",)),
        name="moe",
    )(tb, nv, tg, tok, xr, g2, wg.reshape(ng, epg, d, ff), wu.reshape(ng, epg, d, ff), wd.reshape(ng, epg, ff, d))


def _final_norm_kernel(x_ref, g_ref, o_ref):
    o_ref[...] = _rms(x_ref[...]) * g_ref[...]


def _final_norm(x, n, g, tm):
    d = x.shape[1]
    return pl.pallas_call(
        _final_norm_kernel,
        grid=(n // tm,),
        in_specs=[pl.BlockSpec((tm, d), lambda i: (i, 0)), pl.BlockSpec((1, d), lambda i: (0, 0))],
        out_specs=pl.BlockSpec((tm, d), lambda i: (i, 0)),
        out_shape=jax.ShapeDtypeStruct((n, d), F32),
        compiler_params=pltpu.CompilerParams(dimension_semantics=("arbitrary",)),
        name="final_norm",
    )(x, g.reshape(1, d))


def _perm_w_in(w_in, dims, m_heads, f_heads, prompt, dtype):
    lw, mw, fw = dims
    o_g = 2 * lw + 4 * mw
    o_q = o_g + 2 * m_heads
    o_k = o_q + fw
    o_v = o_k + fw
    o_ff = o_v + fw
    n_gate = 2 * m_heads + f_heads
    gates = jnp.concatenate([w_in[:, o_g:o_q], w_in[:, o_ff:o_ff + f_heads]], axis=1)
    gates = jnp.pad(gates, ((0, 0), (0, GATE_W - n_gate)))
    if prompt:
        w = jnp.concatenate([w_in[:, :o_g], w_in[:, o_k:o_v], gates], axis=1)
        return w.astype(dtype), w_in[:, o_q:o_ff].T.astype(dtype)
    w = jnp.concatenate([w_in[:, :o_g], w_in[:, o_q:o_k], gates, w_in[:, o_k:o_ff]], axis=1)
    return w.astype(dtype), None


def _route_weights(rg, rgb, re, reb):
    n = rg.shape[1] + re.shape[1]
    w = jnp.pad(jnp.concatenate([rg, re], axis=1), ((0, 0), (0, ROUTE_W - n)))
    b = jnp.pad(jnp.concatenate([rgb, reb]), (0, ROUTE_W - n)).reshape(1, ROUTE_W)
    return w, b


def _lru_gate_weights(wr, wi, dtype):
    nb, bs, _ = wr.shape
    eye = jnp.eye(nb, dtype=wr.dtype)

    def dense(w):
        return jnp.einsum("ncd,nm->ncmd", w, eye).reshape(nb * bs, nb * bs)

    return jnp.concatenate([dense(wr), dense(wi)], axis=1).astype(dtype)


def _tile(n, preferred):
    return preferred if n % preferred == 0 else n


def kernel(x_prompt, x_sample, cache_k, cache_v, cache_logf, page_table, state_conv, state_lru_h, state_mlstm_C, state_mlstm_n, state_mlstm_m, norm1, w_in, conv_w, conv_b, lru_wr, lru_br, lru_wi, lru_bi, lru_lambda, m_bi, m_bf, fox_bf, out_norm, w_out, norm2, router_g, router_g_b, router_e, router_e_b, w_gate, w_up, w_down, final_norm):
    depth, d = norm1.shape
    lw = conv_w.shape[-1]
    mh = m_bi.shape[-1]
    fh = fox_bf.shape[-1]
    dh = HEAD_DIM
    mw, fw = mh * dh, fh * dh
    dims = (lw, mw, fw)
    n_gate = 2 * mh + fh
    pool, page = cache_k.shape[1], cache_k.shape[2]
    ck = cache_k.transpose(0, 1, 3, 4, 2).reshape(depth, pool, fw, page)
    cv = cache_v.transpose(0, 1, 3, 4, 2).reshape(depth, pool, fw, page)
    clf = cache_logf.transpose(0, 1, 3, 2)

    def layer(x2, b, t, l, conv_buf, h0, c0, n0, m0, prompt):
        n = b * t
        hp = (not prompt) and l < depth - 1
        wdt = F32 if hp else BF16
        w_row, w_t = _perm_w_in(w_in[l], dims, mh, fh, prompt, wdt)
        if prompt:
            row_outs = [(2 * lw, F32, 1.0), (4 * mw, F32, 1.0), (fw, BF16, 1.0), (GATE_W, F32, 1.0)]
            t_outs = [(fw, (BF16,), LOG2E * dh ** -0.5), (fw, (F32,), 1.0), (fw, (F32, BF16), 1.0)]
            lru, mz, kb, gate, qt, kt, vt, vtb = _in_proj(x2, n, norm1[l], w_row, w_t, row_outs, t_outs,
                                                          _tile(n, 512), t, hp)
        else:
            row_outs = [(2 * lw, F32, 1.0), (4 * mw, F32, 1.0), (fw, wdt, dh ** -0.5), (GATE_W, F32, 1.0),
                        (fw, F32, 1.0), (fw, F32, 1.0)]
            lru, mz, fq, gate, k_new, v_new = _in_proj(x2, n, norm1[l], w_row, w_t, row_outs, [], _tile(n, 512),
                                                       t, hp)

        tp = -(-t // CHUNK) * CHUNK
        gate3 = gate.reshape(b, t, GATE_W)
        if tp != t:
            gate3 = jnp.pad(gate3, ((0, 0), (0, tp - t), (0, 0)))
        gbias = jnp.pad(jnp.concatenate([m_bi[l], m_bf[l], fox_bf[l]]), (0, GATE_W - n_gate)).reshape(1, GATE_W)
        act, cum, act_t, cum_t = _gates(gate3, gbias, mh, 2 * mh, 2 * SUBLANES, _tile(tp, 4 * CHUNK))

        cbuf8 = jnp.pad(conv_buf, ((0, 0), (SUBLANES - (CONV_W - 1), 0), (0, 0)))
        wg = _lru_gate_weights(lru_wr[l], lru_wi[l], wdt)
        bg = jnp.concatenate([lru_br[l], lru_bi[l]]).reshape(1, 2 * lw)
        y_l, cn8, hn = _lru(lru.reshape(b, t, 2 * lw), cbuf8, h0.reshape(b, 1, lw), conv_w[l],
                            conv_b[l].reshape(1, lw), wg, bg, lru_lambda[l].reshape(1, lw), _tile(t, 256), hp)

        y_m, c_new, n_new, m_new = _mlstm(mz.reshape(b, t, 4 * mw), act, cum, act_t, cum_t, c0,
                                          n0.reshape(b, mh, 1, dh), m0.reshape(b, mh, 1, 1), mh, hp)

        if prompt:
            tq = _tile(t, 256)
            y_f = _fox_prompt(qt, kb.reshape(b, t, fw), vtb, cum, 2 * mh, tq, tq)
            k_state = kt.reshape(b, fh, dh, t).transpose(0, 3, 1, 2)
            v_state = vt.reshape(b, fh, dh, t).transpose(0, 3, 1, 2)
        else:
            g_t = cum_t[:, 2 * mh:2 * mh + fh, :t]
            y_f = _fox_decode(page_table, fq.reshape(b, t, fw), k_new.reshape(b, t, fw), v_new.reshape(b, t, fw),
                              g_t, ck, cv, clf, l, 16, hp)
            k_state = k_new.reshape(b, t, fh, dh)
            v_state = v_new.reshape(b, t, fh, dh)

        w_route, b_route = _route_weights(router_g[l], router_g_b[l], router_e[l], router_e_b[l])
        xr = _out_proj(x2, n, y_l.reshape(n, lw), y_m.reshape(n, mw), y_f.reshape(n, fw),
                       out_norm[l].reshape(1, d), w_out[l].astype(wdt), norm2[l].reshape(1, d),
                       w_route, b_route, _tile(n, 512), hp)
        x_next = _moe(xr, n, norm2[l].reshape(1, d), w_gate[l].astype(BF16), w_up[l].astype(BF16),
                      w_down[l].astype(BF16), 256 if n >= 4096 else 64)
        lf_state = act_t[:, 2 * mh:2 * mh + fh, :t].transpose(0, 2, 1)
        state = (k_state, v_state, lf_state, cn8[:, SUBLANES - (CONV_W - 1):], hn[:, 0], c_new,
                 n_new.reshape(b, mh, dh), m_new.reshape(b, mh))
        return x_next, state

    bp = x_prompt.shape[0]
    zc = jnp.zeros((bp, CONV_W - 1, lw), F32)
    zh = jnp.zeros((bp, lw), F32)
    z_c = jnp.zeros((bp, mh, dh, dh), F32)
    zn = jnp.zeros((bp, mh, dh), F32)
    zm = jnp.zeros((bp, mh), F32)

    bs, ts = x_sample.shape[0], x_sample.shape[1]
    tp = x_prompt.shape[1]
    xp, xs = x_prompt.reshape(bp * tp, d), x_sample.reshape(bs * ts, d)
    st_p, st_s = [], []
    for l in range(depth):
        xp, sp = layer(xp, bp, tp, l, zc, zh, z_c, zn, zm, True)
        xs, ss = layer(xs, bs, ts, l, state_conv[l], state_lru_h[l], state_mlstm_C[l], state_mlstm_n[l],
                       state_mlstm_m[l], False)
        st_p.append(sp)
        st_s.append(ss)

    def fin(x, b, t):
        return _final_norm(x, b * t, final_norm, _tile(b * t, 512)).reshape(b, t, d)

    def stk(states, i):
        return jnp.stack([s[i] for s in states])

    return ((fin(xp, bp, tp), fin(xs, bs, ts)) + tuple(stk(st_p, i) for i in range(8)) + tuple(stk(st_s, i) for i in range(8)))
```

```python
import functools
import math

import jax
import jax.numpy as jnp
from jax import lax
from jax.experimental import pallas as pl
from jax.experimental.pallas import tpu as pltpu

F32 = jnp.float32
BF16 = jnp.bfloat16

HEAD_DIM = 64
LRU_C = 8.0
CONV_W = 4
CHUNK = 128
N_GROUPS = 4
EXPERTS_PER_GROUP = 4
EPS = 1e-6
NEG = -1e30
LANES = 128
SUBLANES = 8
GATE_W = LANES

LOG2E = 1.4426950408889634


def _dot(a, b):
    return jnp.dot(a, b, preferred_element_type=F32)


def _dot_nt(a, b):
    return lax.dot_general(a, b, (((1,), (1,)), ((), ())), preferred_element_type=F32)


def _split(a):
    hi = a.astype(BF16)
    return hi, (a - hi.astype(F32)).astype(BF16)


def _mm(a, b, hp, nt=False):
    dot = _dot_nt if nt else _dot
    if not hp:
        return dot(a.astype(BF16), b.astype(BF16))
    ah, al = _split(a)
    bh, bl = _split(b)
    return dot(ah, bh) + (dot(al, bh) + dot(ah, bl))


def _mm_sel(a, sel, sel_left=False):
    hi = a.astype(BF16)
    r1 = a - hi.astype(F32)
    mid = r1.astype(BF16)
    lo = (r1 - mid.astype(F32)).astype(BF16)
    sel = sel.astype(BF16)
    if sel_left:
        return _dot(sel, hi) + (_dot(sel, mid) + _dot(sel, lo))
    return _dot(hi, sel) + (_dot(mid, sel) + _dot(lo, sel))


def _log_sigmoid(x):
    return jnp.minimum(x, 0.0) - jnp.log1p(jnp.exp(-jnp.abs(x)))


def _softplus(x):
    return jnp.maximum(x, 0.0) + jnp.log1p(jnp.exp(-jnp.abs(x)))


def _gelu_tanh(x):
    return 0.5 * x * (1.0 + jnp.tanh(math.sqrt(2.0 / math.pi) * (x + 0.044715 * (x * x * x))))


def _rms(x):
    return x * lax.rsqrt(jnp.mean(x * x, axis=-1, keepdims=True) + EPS)


def _in_proj_kernel(x_ref, g_ref, w_ref, *rest, row_spec, t_spec, hp):
    if t_spec:
        wt_ref, outs = rest[0], rest[1:]
    else:
        wt_ref, outs = None, rest
    xn = _rms(x_ref[...]) * g_ref[...]
    if not hp:
        xn = xn.astype(BF16)
    k = 0
    off = 0
    for width, scale in row_spec:
        r = _mm(xn, w_ref[:, off:off + width], hp)
        off += width
        if scale != 1.0:
            r = r * scale
        outs[k][...] = r.astype(outs[k].dtype)
        k += 1
    off = 0
    for rows, scale, copies in t_spec:
        r = _mm(wt_ref[off:off + rows, :], xn, hp, nt=True)
        off += rows
        if scale != 1.0:
            r = r * scale
        for _ in range(copies):
            outs[k][...] = r.astype(outs[k].dtype)
            k += 1


def _in_proj(x, n, g, w, w_t, row_outs, t_outs, tm, seq, hp):
    d = x.shape[1]
    row = lambda i: (i, 0)
    fixed = lambda i: (0, 0)
    per_b = seq // tm
    tmap = lambda i: (i // per_b, 0, i % per_b)
    out_specs = [pl.BlockSpec((tm, wd), row) for wd, _, _ in row_outs]
    out_shape = [jax.ShapeDtypeStruct((n, wd), dt) for wd, dt, _ in row_outs]
    ins = [x, g.reshape(1, d), w]
    in_specs = [pl.BlockSpec((tm, d), row), pl.BlockSpec((1, d), fixed), pl.BlockSpec(w.shape, fixed)]
    if t_outs:
        ins.append(w_t)
        in_specs.append(pl.BlockSpec(w_t.shape, fixed))
    for rows, dts, _ in t_outs:
        out_specs += [pl.BlockSpec((None, rows, tm), tmap)] * len(dts)
        out_shape += [jax.ShapeDtypeStruct((n // seq, rows, seq), dt) for dt in dts]
    return pl.pallas_call(
        functools.partial(_in_proj_kernel, row_spec=tuple((wd, sc) for wd, _, sc in row_outs),
                          t_spec=tuple((rows, sc, len(dts)) for rows, dts, sc in t_outs), hp=hp),
        grid=(n // tm,),
        in_specs=in_specs,
        out_specs=out_specs,
        out_shape=out_shape,
        compiler_params=pltpu.CompilerParams(dimension_semantics=("arbitrary",)),
        name="in_proj",
    )(*ins)


def _gates_kernel(z_ref, b_ref, act_ref, cum_ref, actT_ref, cumT_ref, carry_ref, *, n_in, n_local):
    c = pl.program_id(1)
    L = CHUNK
    rows = actT_ref.shape[0]

    @pl.when(c == 0)
    def _():
        carry_ref[...] = jnp.zeros_like(carry_ref)

    r = lax.broadcasted_iota(jnp.int32, (L, L), 0)
    s = lax.broadcasted_iota(jnp.int32, (L, L), 1)
    tri = jnp.where(s <= r, 1.0, 0.0)
    carry = carry_ref[0:1, :]
    for u in range(z_ref.shape[0] // L):
        lo = u * L
        pre = z_ref[lo:lo + L, :] + b_ref[...]
        lane = lax.broadcasted_iota(jnp.int32, pre.shape, 1)
        act = jnp.where(lane < n_in, pre, _log_sigmoid(pre))
        cum = _mm_sel(act, tri, sel_left=True) + jnp.where(lane[0:1, :] >= n_local, carry, 0.0)
        carry = cum[L - 1:L, :]
        act_ref[lo:lo + L, :] = act
        cum_ref[lo:lo + L, :] = cum
        actT_ref[:, lo:lo + L] = act.T[0:rows, :]
        cumT_ref[:, lo:lo + L] = cum.T[0:rows, :]
    carry_ref[...] = jnp.broadcast_to(carry, carry_ref.shape)


def _gates(z, bias, n_in, n_local, rows, tt):
    b, t, gw = z.shape
    blk = pl.BlockSpec((None, tt, gw), lambda i, c: (i, c, 0))
    blk_t = pl.BlockSpec((None, rows, tt), lambda i, c: (i, 0, c))
    return pl.pallas_call(
        functools.partial(_gates_kernel, n_in=n_in, n_local=n_local),
        grid=(b, t // tt),
        in_specs=[blk, pl.BlockSpec((1, gw), lambda i, c: (0, 0))],
        out_specs=[blk, blk, blk_t, blk_t],
        out_shape=[jax.ShapeDtypeStruct((b, t, gw), F32)] * 2 + [jax.ShapeDtypeStruct((b, rows, t), F32)] * 2,
        scratch_shapes=[pltpu.VMEM((SUBLANES, gw), F32)],
        compiler_params=pltpu.CompilerParams(dimension_semantics=("arbitrary", "arbitrary")),
        name="gates",
    )(z, bias)


def _lru_kernel(u_ref, cbuf_ref, h0_ref, cw_ref, cb_ref, wg_ref, bg_ref, lam_ref,
                y_ref, cnew_ref, hnew_ref, ext_ref, hc_ref, *, hp):
    t = pl.program_id(1)
    nt = pl.num_programs(1)
    tt = y_ref.shape[0]
    c = y_ref.shape[1]
    pad = SUBLANES

    @pl.when(t == 0)
    def _():
        ext_ref[0:pad, :] = cbuf_ref[...]
        hc_ref[...] = jnp.broadcast_to(h0_ref[...], hc_ref.shape)

    u = u_ref[:, 0:c]
    gl = u_ref[:, c:2 * c]
    ext_ref[pad:pad + tt, :] = u
    ext = ext_ref[...]
    xc = cw_ref[CONV_W - 1:CONV_W, :] * u + cb_ref[...]
    for j in range(CONV_W - 1):
        xc = xc + cw_ref[j:j + 1, :] * pltpu.roll(ext, CONV_W - 1 - j, 0)[pad:pad + tt, :]
    ext_ref[0:pad, :] = ext_ref[tt:tt + pad, :]

    pre = _mm(xc, wg_ref[...], hp) + bg_ref[...]
    r = jax.nn.sigmoid(pre[:, 0:c])
    i = jax.nn.sigmoid(pre[:, c:2 * c])
    log_a = (-LRU_C) * r * _softplus(-lam_ref[...])
    a = jnp.exp(log_a)
    th = jnp.tanh(log_a)
    bx = jnp.sqrt(-2.0 * th / (1.0 - th)) * (i * xc)

    row = lax.broadcasted_iota(jnp.int32, (tt, c), 0) & (SUBLANES - 1)
    s = 1
    while s < SUBLANES:
        keep = row >= s
        bx = jnp.where(keep, a * pltpu.roll(bx, s, 0) + bx, bx)
        a = jnp.where(keep, a * pltpu.roll(a, s, 0), a)
        s *= 2
    carry = hc_ref[...]
    gate = _gelu_tanh(gl)
    for g in range(tt // SUBLANES):
        lo = g * SUBLANES
        h = a[lo:lo + SUBLANES, :] * carry + bx[lo:lo + SUBLANES, :]
        y_ref[lo:lo + SUBLANES, :] = h * gate[lo:lo + SUBLANES, :]
        carry = jnp.broadcast_to(h[SUBLANES - 1:SUBLANES, :], (SUBLANES, c))
    hc_ref[...] = carry

    @pl.when(t == nt - 1)
    def _():
        cnew_ref[...] = ext_ref[0:pad, :]
        hnew_ref[...] = carry[0:1, :]


def _lru(u, cbuf8, h0, cw, cb, wg, bg, lam, tt, hp):
    b, t, c2 = u.shape
    c = c2 // 2
    fixed = lambda i, j: (0, 0)
    per_b = lambda i, j: (i, 0, 0)
    return pl.pallas_call(
        functools.partial(_lru_kernel, hp=hp),
        grid=(b, t // tt),
        in_specs=[
            pl.BlockSpec((None, tt, c2), lambda i, j: (i, j, 0)),
            pl.BlockSpec((None, SUBLANES, c), per_b),
            pl.BlockSpec((None, 1, c), per_b),
            pl.BlockSpec((CONV_W, c), fixed),
            pl.BlockSpec((1, c), fixed),
            pl.BlockSpec((c, c2), fixed),
            pl.BlockSpec((1, c2), fixed),
            pl.BlockSpec((1, c), fixed),
        ],
        out_specs=[
            pl.BlockSpec((None, tt, c), lambda i, j: (i, j, 0)),
            pl.BlockSpec((None, SUBLANES, c), per_b),
            pl.BlockSpec((None, 1, c), per_b),
        ],
        out_shape=[
            jax.ShapeDtypeStruct((b, t, c), F32),
            jax.ShapeDtypeStruct((b, SUBLANES, c), F32),
            jax.ShapeDtypeStruct((b, 1, c), F32),
        ],
        scratch_shapes=[pltpu.VMEM((tt + SUBLANES, c), F32), pltpu.VMEM((SUBLANES, c), F32)],
        compiler_params=pltpu.CompilerParams(dimension_semantics=("arbitrary", "arbitrary")),
        name="rg_lru",
    )(u, cbuf8, h0, cw, cb, wg, bg, lam)


def _mlstm_kernel(z_ref, act_ref, cum_ref, actT_ref, cumT_ref, c0_ref, n0_ref, m0_ref,
                  y_ref, c_ref, n_ref, m_ref, *, heads, tv, hp):
    ck = pl.program_id(1)
    L = act_ref.shape[0]
    dh = HEAD_DIM
    w = heads * dh

    @pl.when(ck == 0)
    def _():
        c_ref[...] = c0_ref[...]
        n_ref[...] = n0_ref[...]
        m_ref[...] = m0_ref[...]

    t_idx = lax.broadcasted_iota(jnp.int32, (L, L), 0)
    s_idx = lax.broadcasted_iota(jnp.int32, (L, L), 1)
    causal = s_idx <= t_idx
    col_ok = lax.broadcasted_iota(jnp.int32, (L, 1), 0) < tv
    row_ok = lax.broadcasted_iota(jnp.int32, (1, L), 1) < tv

    def rows(ref, lo):
        x = ref[:, lo:lo + dh]
        if tv < L:
            x = jnp.concatenate([x, jnp.zeros((L - tv, dh), F32)], axis=0)
        return x

    hs = range(heads)
    q, k, v, st = [], [], [], []
    for h in hs:
        qh = rows(z_ref, h * dh)
        q.append(qh)
        k.append(rows(z_ref, w + h * dh) * (dh ** -0.5))
        vh = rows(z_ref, 2 * w + h * dh)
        v.append(vh if hp else vh.astype(BF16))
    qb = [x if hp else x.astype(BF16) for x in q]
    qk = [_mm(qb[h], k[h], hp, nt=True) for h in hs]
    qc = [_mm(qb[h], c_ref[h], hp) for h in hs]
    for h in hs:
        ic_col = act_ref[:, h:h + 1]
        bc_col = cum_ref[:, heads + h:heads + h + 1]
        ic_row = actT_ref[h:h + 1, :]
        bc_row = cumT_ref[heads + h:heads + h + 1, :]
        bc_last = bc_col[tv - 1:tv, :]
        if tv < L:
            ic_col = jnp.where(col_ok, ic_col, NEG)
            ic_row = jnp.where(row_ok, ic_row, NEG)
            bc_col = jnp.where(col_ok, bc_col, bc_last)
            bc_row = jnp.where(row_ok, bc_row, bc_last)
        m_prev = m_ref[h]
        dlog = jnp.where(causal, bc_col - bc_row + ic_row, NEG)
        inter = bc_col + m_prev
        mt = jnp.maximum(inter, jnp.max(dlog, axis=1, keepdims=True))
        sqk = qk[h] * jnp.exp(dlog - mt)
        m_new = mt[L - 1:L, :]
        kw = k[h] * jnp.exp(bc_last - bc_col + ic_col - m_new)
        st.append((sqk, jnp.exp(inter - mt), mt, m_new, jnp.exp(bc_last + m_prev - m_new), kw))
    sv = [_mm(st[h][0], v[h], hp) for h in hs]
    kv = [_mm(st[h][5].T, v[h], hp) for h in hs]
    for h in hs:
        sqk, decay, mt, m_new, g, kw = st[h]
        n_prev = n_ref[h]
        num = sv[h] + decay * qc[h]
        den = jnp.sum(sqk, axis=1, keepdims=True) + decay * jnp.sum(q[h] * n_prev, axis=1, keepdims=True)
        hh = num / jnp.maximum(jnp.abs(den), jnp.exp(-mt))
        c_ref[h] = g * c_ref[h] + kv[h]
        n_ref[h] = g * n_prev + jnp.sum(kw, axis=0, keepdims=True)
        m_ref[h] = m_new
        o = z_ref[:, 3 * w + h * dh:3 * w + (h + 1) * dh]
        y_ref[:, h * dh:(h + 1) * dh] = jax.nn.sigmoid(o) * hh[0:tv, :]


def _mlstm(z, act, cum, actT, cumT, c0, n0, m0, heads, hp):
    b, t, w4 = z.shape
    w = w4 // 4
    L = CHUNK
    tv = min(t, L)
    nc = act.shape[1] // L
    gw = act.shape[-1]
    rows = actT.shape[1]
    dh = HEAD_DIM
    tok = lambda i, c: (i, c, 0)
    tokT = lambda i, c: (i, 0, c)
    st = lambda i, c: (i, 0, 0, 0)
    st_specs = [pl.BlockSpec((None, heads, dh, dh), st), pl.BlockSpec((None, heads, 1, dh), st),
                pl.BlockSpec((None, heads, 1, 1), st)]
    return pl.pallas_call(
        functools.partial(_mlstm_kernel, heads=heads, tv=tv, hp=hp),
        grid=(b, nc),
        in_specs=[pl.BlockSpec((None, tv, w4), tok), pl.BlockSpec((None, L, gw), tok), pl.BlockSpec((None, L, gw), tok),
                  pl.BlockSpec((None, rows, L), tokT), pl.BlockSpec((None, rows, L), tokT)] + st_specs,
        out_specs=[pl.BlockSpec((None, tv, w), tok)] + st_specs,
        out_shape=[jax.ShapeDtypeStruct((b, t, w), F32), jax.ShapeDtypeStruct(c0.shape, F32),
                   jax.ShapeDtypeStruct(n0.shape, F32), jax.ShapeDtypeStruct(m0.shape, F32)],
        compiler_params=pltpu.CompilerParams(dimension_semantics=("arbitrary", "arbitrary")),
        name="mlstm",
    )(z, act, cum, actT, cumT, c0, n0, m0)


def _fox_prompt_kernel(qt_ref, k_ref, vt_ref, cum_ref, y_ref, fb_ref, xs_ref, ml_ref, acc_ref, *, f_lane0, tk):
    i = pl.program_id(1)
    w, tq = qt_ref.shape
    dh = HEAD_DIM
    heads = w // dh
    sub = tq // tk
    reps = tq // LANES

    @pl.when(i == 0)
    def _():
        cum = cum_ref[...]
        lane = lax.broadcasted_iota(jnp.int32, cum.shape, 1)
        for h in range(heads):
            col = jnp.sum(jnp.where(lane == f_lane0 + h, cum, 0.0), axis=1, keepdims=True)
            fb_ref[h] = jnp.broadcast_to(col * LOG2E, fb_ref.shape[1:])

    row = lax.broadcasted_iota(jnp.int32, (LANES, tq), 0)
    qm = []
    for h in range(heads):
        qt = qt_ref[(h // 2) * LANES:(h // 2 + 1) * LANES, :]
        keep = (row < dh) if h % 2 == 0 else (row >= dh)
        qm.append(jnp.where(keep, qt, jnp.zeros_like(qt)))

    def scores(start, slot):
        for h in range(heads):
            kb = k_ref[pl.ds(start, tk), (h // 2) * LANES:(h // 2 + 1) * LANES]
            fb = fb_ref[h, pl.ds(start, tk), :]
            xs_ref[slot, h] = _dot(kb, qm[h]) - jnp.concatenate([fb] * reps, axis=1)

    def update(slot, start, diag_off):
        for h in range(heads):
            x = xs_ref[slot, h]
            if diag_off is not None:
                r = lax.broadcasted_iota(jnp.int32, x.shape, 0) + diag_off
                c = lax.broadcasted_iota(jnp.int32, x.shape, 1)
                x = jnp.where(r <= c, x, NEG)
            m = ml_ref[h, 0:1, :]
            m_new = jnp.maximum(m, jnp.max(x, axis=0, keepdims=True))
            alpha = jnp.exp2(m - m_new)
            pr = jnp.exp2(x - m_new)
            ml_ref[h, 0:1, :] = m_new
            ml_ref[h, 1:2, :] = alpha * ml_ref[h, 1:2, :] + jnp.sum(pr, axis=0, keepdims=True)
            vb = vt_ref[h * dh:(h + 1) * dh, pl.ds(start, tk)]
            acc_ref[h] = alpha * acc_ref[h] + _dot(vb, pr.astype(BF16))

    for h in range(heads):
        ml_ref[h, 0:1, :] = jnp.full((1, tq), NEG, F32)
        ml_ref[h, 1:2, :] = jnp.zeros((1, tq), F32)
    acc_ref[...] = jnp.zeros_like(acc_ref)
    n = i * sub
    scores(0, 0)

    @pl.loop(0, n)
    def _(j):
        scores(pl.multiple_of((j + 1) * tk, tk), (j + 1) % 2)
        update(j % 2, pl.multiple_of(j * tk, tk), None)

    for u in range(sub):
        if u + 1 < sub:
            scores(pl.multiple_of(i * tq + (u + 1) * tk, tk), (n + u + 1) % 2)
        update((n + u) % 2, pl.multiple_of(i * tq + u * tk, tk), u * tk)
    y_ref[...] = jnp.concatenate([acc_ref[h] / ml_ref[h, 1:2, :] for h in range(heads)], axis=0).T


def _fox_prompt(qt, k, vt, cum, f_lane0, tq, tk):
    b, w, s = qt.shape
    heads = w // HEAD_DIM
    return pl.pallas_call(
        functools.partial(_fox_prompt_kernel, f_lane0=f_lane0, tk=tk),
        grid=(b, s // tq),
        in_specs=[
            pl.BlockSpec((None, w, tq), lambda bi, i: (bi, 0, i)),
            pl.BlockSpec((None, s, w), lambda bi, i: (bi, 0, 0)),
            pl.BlockSpec((None, w, s), lambda bi, i: (bi, 0, 0)),
            pl.BlockSpec((None, s, cum.shape[-1]), lambda bi, i: (bi, 0, 0)),
        ],
        out_specs=pl.BlockSpec((None, tq, w), lambda bi, i: (bi, i, 0)),
        out_shape=jax.ShapeDtypeStruct((b, s, w), F32),
        scratch_shapes=[pltpu.VMEM((heads, s, LANES), F32), pltpu.VMEM((2, heads, tk, tq), F32),
                        pltpu.VMEM((heads, SUBLANES, tq), F32), pltpu.VMEM((heads, HEAD_DIM, tq), F32)],
        compiler_params=pltpu.CompilerParams(dimension_semantics=("arbitrary", "arbitrary")),
        name="fox_prompt",
    )(qt, k, vt, cum)


DECODE_GROUPS = 2


def _fox_decode_kernel(pt_ref, q_ref, kn_ref, vn_ref, gt_ref, *rest, pps, heads, hp):
    k_refs = rest[0:pps]
    v_refs = rest[pps:2 * pps]
    lf_refs = rest[2 * pps:3 * pps]
    y_ref = rest[3 * pps]
    qh_ref, ql_ref, m_ref, l_ref, acc_ref, car_ref = rest[3 * pps + 1:]
    j = pl.program_id(1)
    nj = pl.num_programs(1)
    t_new = q_ref.shape[0]
    w = q_ref.shape[1]
    dh = HEAD_DIM
    page = k_refs[0].shape[1]
    rows = heads * t_new
    lane_head = lax.broadcasted_iota(jnp.int32, (t_new, w), 1) // dh

    @pl.when(j == 0)
    def _():
        q = q_ref[...].astype(F32)
        qh, ql = _split(q)
        for h in range(heads):
            keep = lane_head == h
            qh_ref[h * t_new:(h + 1) * t_new, :] = jnp.where(keep, qh, jnp.zeros_like(qh))
            ql_ref[h * t_new:(h + 1) * t_new, :] = jnp.where(keep, ql, jnp.zeros_like(ql))
        m_ref[...] = jnp.full(m_ref.shape, NEG, F32)
        l_ref[...] = jnp.zeros_like(l_ref)
        acc_ref[...] = jnp.zeros_like(acc_ref)
        car_ref[...] = jnp.zeros_like(car_ref)

    qh = qh_ref[...]
    ql = ql_ref[...]
    qhl = jnp.concatenate([qh, ql], axis=0)

    def qk(k, nt=False):
        dot = _dot_nt if nt else _dot
        if not hp:
            return dot(qh, k.astype(BF16))
        kh, kl = _split(k)
        both = dot(qhl, kh)
        return both[0:rows, :] + (both[rows:2 * rows, :] + dot(qh, kl))

    def pv(p, v, nt):
        dot = _dot_nt if nt else _dot
        if not hp:
            return dot(p.astype(BF16), v.astype(BF16))
        ph, pl_ = _split(p)
        vh, vl = _split(v)
        both = dot(jnp.concatenate([ph, pl_], axis=0), vh)
        return both[0:rows, :] + (both[rows:2 * rows, :] + dot(ph, vl))

    def expand(r):
        return jnp.concatenate([jnp.broadcast_to(r[h:h + 1, :], (t_new, r.shape[1])) for h in range(heads)], axis=0)

    lft = jnp.concatenate([lf_refs[r][...] for r in range(pps)], axis=0)
    jj = lax.broadcasted_iota(jnp.int32, (page, 2 * page), 0)
    ss = lax.broadcasted_iota(jnp.int32, (page, 2 * page), 1)
    u_aug = jnp.where((jj > ss) | (ss >= page), 1.0, 0.0)
    suf = _mm_sel(lft, u_aug)
    carry = car_ref[...]
    xs = []
    for r in range(pps):
        r_loc = suf[r * heads:(r + 1) * heads, 0:page] + carry
        carry = carry + suf[r * heads:(r + 1) * heads, page:2 * page]
        xs.append(qk(k_refs[r][...]) + expand(r_loc))
    car_ref[...] = carry

    m_new = m_ref[...]
    l = l_ref[...]
    acc = acc_ref[...]
    per = pps // DECODE_GROUPS
    for gi in range(DECODE_GROUPS):
        grp = range(gi * per, (gi + 1) * per)
        m_prev = m_new
        for r in grp:
            m_new = jnp.maximum(m_new, jnp.max(xs[r], axis=1, keepdims=True))
        alpha = jnp.exp(m_prev - m_new)
        l = alpha * l
        acc = alpha * acc
        for r in grp:
            p = jnp.exp(xs[r] - m_new)
            l = l + jnp.sum(p, axis=1, keepdims=True)
            acc = acc + pv(p, v_refs[r][...], True)
    m_ref[...] = m_new
    l_ref[...] = l
    acc_ref[...] = acc

    @pl.when(j == nj - 1)
    def _():
        x = qk(kn_ref[...], nt=True) - expand(gt_ref[...])
        tq = lax.broadcasted_iota(jnp.int32, (rows, t_new), 0) % t_new
        sk = lax.broadcasted_iota(jnp.int32, (rows, t_new), 1)
        x = jnp.where(sk <= tq, x, NEG)
        m_fin = jnp.maximum(m_new, jnp.max(x, axis=1, keepdims=True))
        a2 = jnp.exp(m_new - m_fin)
        p = jnp.exp(x - m_fin)
        l_fin = a2 * l + jnp.sum(p, axis=1, keepdims=True)
        o = (a2 * acc + pv(p, vn_ref[...], False)) / l_fin
        out = jnp.zeros((t_new, w), F32)
        for h in range(heads):
            out = jnp.where(lane_head == h, o[h * t_new:(h + 1) * t_new, :], out)
        y_ref[...] = out


def _fox_decode(page_table, q, k_new, v_new, g_t, cache_k, cache_v, cache_lf, layer, pps, hp):
    b, t_new, w = q.shape
    heads = cache_lf.shape[2]
    page = cache_k.shape[3]
    n_pages = page_table.shape[1]
    steps = n_pages // pps
    rows = heads * t_new

    def page_map(r):
        return lambda bi, j, pt: (layer, pt[bi, n_pages - 1 - (j * pps + r)], 0, 0)

    tok = lambda bi, j, pt: (bi, 0, 0)
    in_specs = [pl.BlockSpec((None, t_new, w), tok), pl.BlockSpec((None, t_new, w), tok),
                pl.BlockSpec((None, t_new, w), tok), pl.BlockSpec((None, heads, t_new), tok)]
    in_specs += [pl.BlockSpec((None, None, w, page), page_map(r)) for r in range(pps)]
    in_specs += [pl.BlockSpec((None, None, w, page), page_map(r)) for r in range(pps)]
    in_specs += [pl.BlockSpec((None, None, heads, page), page_map(r)) for r in range(pps)]
    grid_spec = pltpu.PrefetchScalarGridSpec(
        num_scalar_prefetch=1,
        grid=(b, steps),
        in_specs=in_specs,
        out_specs=pl.BlockSpec((None, t_new, w), tok),
        scratch_shapes=[pltpu.VMEM((rows, w), BF16), pltpu.VMEM((rows, w), BF16),
                        pltpu.VMEM((rows, 1), F32), pltpu.VMEM((rows, 1), F32),
                        pltpu.VMEM((rows, w), F32), pltpu.VMEM((heads, page), F32)],
    )
    return pl.pallas_call(
        functools.partial(_fox_decode_kernel, pps=pps, heads=heads, hp=hp),
        grid_spec=grid_spec,
        out_shape=jax.ShapeDtypeStruct((b, t_new, w), F32),
        compiler_params=pltpu.CompilerParams(dimension_semantics=("arbitrary", "arbitrary")),
        name="fox_decode",
    )(page_table, q, k_new, v_new, g_t, *([cache_k] * pps), *([cache_v] * pps), *([cache_lf] * pps))


ROUTE_W = LANES


def _route(logits):
    lane = lax.broadcasted_iota(jnp.int32, logits.shape, 1).astype(F32)
    big = float(ROUTE_W)

    def first_max(x):
        mx = jnp.max(x, axis=1, keepdims=True)
        return mx, jnp.min(jnp.where(x == mx, lane, big), axis=1, keepdims=True)

    gmask = lane < N_GROUPS
    lg = jnp.where(gmask, logits, NEG)
    mg, gidx = first_max(lg)
    pg_sel = 1.0 / jnp.sum(jnp.where(gmask, jnp.exp(lg - mg), 0.0), axis=1, keepdims=True)
    lo = N_GROUPS + gidx * EXPERTS_PER_GROUP
    le = jnp.where((lane >= lo) & (lane < lo + EXPERTS_PER_GROUP), logits, NEG)
    v1, i1 = first_max(le)
    le2 = jnp.where(lane == i1, NEG, le)
    v2, i2 = first_max(le2)
    e21 = jnp.exp(v2 - v1)
    w1 = pg_sel / (1.0 + e21)
    w2 = w1 * e21
    tile = jnp.where(lane == i1, w1, 0.0) + jnp.where(lane == i2, w2, 0.0)
    tile = jnp.where(lane == 0.0, i1 - N_GROUPS, tile)
    tile = jnp.where(lane == 1.0, i2 - N_GROUPS, tile)
    tile = jnp.where(lane == 2.0, w1, tile)
    tile = jnp.where(lane == 3.0, w2, tile)
    return tile


def _out_proj_kernel(x_ref, yl_ref, ym_ref, yf_ref, go_ref, wo_ref, g2_ref, wr_ref, br_ref,
                     xr_ref, *, hp):
    acc = x_ref[...]
    off = 0
    for y_ref in (yl_ref, ym_ref, yf_ref):
        wd = y_ref.shape[-1]
        yn = _rms(y_ref[...]) * go_ref[:, off:off + wd]
        acc = acc + _mm(yn, wo_ref[off:off + wd, :], hp)
        off += wd
    d = acc.shape[-1]
    xr_ref[:, 0:d] = acc
    xn = _rms(acc) * g2_ref[...]
    xr_ref[:, d:d + ROUTE_W] = _route(_mm(xn, wr_ref[...], True) + br_ref[...])


def _out_proj(x, n, yl, ym, yf, g_out, w_out, g2, w_route, b_route, tm, hp):
    d = x.shape[1]
    row = lambda i: (i, 0)
    fixed = lambda i: (0, 0)
    ins = (x, yl, ym, yf)
    return pl.pallas_call(
        functools.partial(_out_proj_kernel, hp=hp),
        grid=(n // tm,),
        in_specs=[pl.BlockSpec((tm, a.shape[1]), row) for a in ins] + [
            pl.BlockSpec((1, d), fixed), pl.BlockSpec(w_out.shape, fixed), pl.BlockSpec((1, d), fixed),
            pl.BlockSpec(w_route.shape, fixed), pl.BlockSpec((1, ROUTE_W), fixed)],
        out_specs=pl.BlockSpec((tm, d + ROUTE_W), row),
        out_shape=jax.ShapeDtypeStruct((n, d + ROUTE_W), F32),
        compiler_params=pltpu.CompilerParams(dimension_semantics=("arbitrary",)),
        name="out_proj",
    )(x, yl, ym, yf, g_out, w_out, g2, w_route, b_route)


MOE_ISSUE_UNROLL = 8


def _moe_kernel(tb_ref, nv_ref, tg_ref, tok_ref, xr_hbm, g2_ref, wg_ref, wu_ref, wd_ref, out_hbm,
                xbuf, obuf, gsem, ssem, *, tm, d, n):
    i = pl.program_id(0)
    nt = pl.num_programs(0)
    slot = i % 2

    def start_gather(tile, sl):
        def body(c, carry):
            for u in range(MOE_ISSUE_UNROLL):
                r = c * MOE_ISSUE_UNROLL + u
                t = tok_ref[tile * tm + r]
                pltpu.make_async_copy(xr_hbm.at[pl.ds(t, 1), :], xbuf.at[sl, pl.ds(r, 1), :],
                                      gsem.at[sl]).start(priority=u % 2)
            return carry
        lax.fori_loop(0, tm // MOE_ISSUE_UNROLL, body, 0)

    def wait_gather(sl):
        pltpu.make_async_copy(xr_hbm.at[pl.ds(0, tm), :], xbuf.at[sl], gsem.at[sl]).wait()

    def start_scatter(tile, sl):
        spare = n + tb_ref[tile] * tm

        def body(c, carry):
            for u in range(MOE_ISSUE_UNROLL):
                r = c * MOE_ISSUE_UNROLL + u
                t = jnp.where(r < nv_ref[tile], tok_ref[tile * tm + r], spare + r)
                pltpu.make_async_copy(obuf.at[sl, pl.ds(r, 1), :], out_hbm.at[pl.ds(t, 1), :],
                                      ssem.at[sl]).start(priority=u % 2)
            return carry
        lax.fori_loop(0, tm // MOE_ISSUE_UNROLL, body, 0)

    def wait_scatter(sl):
        pltpu.make_async_copy(obuf.at[sl], out_hbm.at[pl.ds(0, tm), :], ssem.at[sl]).wait()

    @pl.when(i == 0)
    def _():
        start_gather(0, 0)
        obuf[0] = jnp.zeros(obuf.shape[1:], F32)
        spare_blocks = (out_hbm.shape[0] - n) // tm
        fills = [pltpu.make_async_copy(obuf.at[0], out_hbm.at[pl.ds(n + blk * tm, tm), :], ssem.at[0])
                 for blk in range(spare_blocks)]
        for f in fills:
            f.start()
        for f in fills:
            f.wait()

    @pl.when(i + 1 < nt)
    def _():
        start_gather(i + 1, 1 - slot)

    wait_gather(slot)

    @pl.when(i >= 2)
    def _():
        wait_scatter(slot)

    x = xbuf[slot, :, 0:d]
    rt = xbuf[slot, :, d:d + ROUTE_W]
    xn = (_rms(x) * g2_ref[...]).astype(BF16)
    lane = lax.broadcasted_iota(jnp.int32, rt.shape, 1)
    first = N_GROUPS + EXPERTS_PER_GROUP * tg_ref[i]
    acc = x
    for j in range(EXPERTS_PER_GROUP):
        h = jax.nn.silu(_dot(xn, wg_ref[j])) * _dot(xn, wu_ref[j])
        gate = jnp.sum(jnp.where(lane == first + j, rt, 0.0), axis=1, keepdims=True)
        acc = acc + gate * _dot(h.astype(BF16), wd_ref[j])
    obuf[slot] = acc
    start_scatter(i, slot)

    @pl.when(i == nt - 1)
    def _():
        wait_scatter(slot)

        @pl.when(i >= 1)
        def _():
            wait_scatter(1 - slot)


def _moe(xr, n, g2, wg, wu, wd, tm):
    d = xr.shape[1] - ROUTE_W
    ne, _, ff = wg.shape
    epg = EXPERTS_PER_GROUP
    ng = ne // epg
    grp = (xr[:n, d] * (1.0 / epg)).astype(jnp.int32)
    onehot = (grp[:, None] == jnp.arange(ng, dtype=jnp.int32)[None, :]).astype(jnp.int32)
    csum = jnp.cumsum(onehot, axis=0)
    cnt = csum[-1]
    rank = jnp.sum((csum - onehot) * onehot, axis=1)
    tiles_g = (cnt + tm - 1) // tm
    tile_end = jnp.cumsum(tiles_g)
    tile_off = tile_end - tiles_g
    pos = (tile_off * tm)[grp] + rank
    n_tiles = n // tm + ng
    tok = jnp.zeros((n_tiles * tm,), jnp.int32).at[pos].set(jnp.arange(n, dtype=jnp.int32), unique_indices=True)
    tile = jnp.arange(n_tiles, dtype=jnp.int32)
    tg = jnp.minimum(jnp.sum((tile[:, None] >= tile_end[None, :]).astype(jnp.int32), axis=1), ng - 1)
    used = tile < tile_end[-1]
    nv = jnp.where(used, jnp.clip(cnt[tg] - (tile - tile_off[tg]) * tm, 0, tm), 0).astype(jnp.int32)
    tb = jnp.where(used, tg, ng + tile - tile_end[-1]).astype(jnp.int32)

    wmap = lambda i, tb_r, nv_r, tg_r, tok_r: (tg_r[i], 0, 0, 0)
    grid_spec = pltpu.PrefetchScalarGridSpec(
        num_scalar_prefetch=4,
        grid=(n_tiles,),
        in_specs=[pl.BlockSpec(memory_space=pl.ANY),
                  pl.BlockSpec((1, d), lambda i, *_: (0, 0)),
                  pl.BlockSpec((None, epg, d, ff), wmap),
                  pl.BlockSpec((None, epg, d, ff), wmap),
                  pl.BlockSpec((None, epg, ff, d), wmap)],
        out_specs=pl.BlockSpec(memory_space=pl.ANY),
        scratch_shapes=[pltpu.VMEM((2, tm, d + ROUTE_W), F32), pltpu.VMEM((2, tm, d), F32),
                        pltpu.SemaphoreType.DMA((2,)), pltpu.SemaphoreType.DMA((2,))],
    )
    return pl.pallas_call(
        functools.partial(_moe_kernel, tm=tm, d=d, n=n),
        grid_spec=grid_spec,
        out_shape=jax.ShapeDtypeStruct((n + 2 * ng * tm, d), F32),
        compiler_params=pltpu.CompilerParams(dimension_semantics=("arbitrary",)),
        name="moe",
    )(tb, nv, tg, tok, xr, g2, wg.reshape(ng, epg, d, ff), wu.reshape(ng, epg, d, ff), wd.reshape(ng, epg, ff, d))


def _final_norm_kernel(x_ref, g_ref, o_ref):
    o_ref[...] = _rms(x_ref[...]) * g_ref[...]


def _final_norm(x, n, g, tm):
    d = x.shape[1]
    return pl.pallas_call(
        _final_norm_kernel,
        grid=(n // tm,),
        in_specs=[pl.BlockSpec((tm, d), lambda i: (i, 0)), pl.BlockSpec((1, d), lambda i: (0, 0))],
        out_specs=pl.BlockSpec((tm, d), lambda i: (i, 0)),
        out_shape=jax.ShapeDtypeStruct((n, d), F32),
        compiler_params=pltpu.CompilerParams(dimension_semantics=("arbitrary",)),
        name="final_norm",
    )(x, g.reshape(1, d))


def _perm_w_in(w_in, dims, m_heads, f_heads, prompt, dtype):
    lw, mw, fw = dims
    o_g = 2 * lw + 4 * mw
    o_q = o_g + 2 * m_heads
    o_k = o_q + fw
    o_v = o_k + fw
    o_ff = o_v + fw
    n_gate = 2 * m_heads + f_heads
    gates = jnp.concatenate([w_in[:, o_g:o_q], w_in[:, o_ff:o_ff + f_heads]], axis=1)
    gates = jnp.pad(gates, ((0, 0), (0, GATE_W - n_gate)))
    if prompt:
        w = jnp.concatenate([w_in[:, :o_g], w_in[:, o_k:o_v], gates], axis=1)
        return w.astype(dtype), w_in[:, o_q:o_ff].T.astype(dtype)
    w = jnp.concatenate([w_in[:, :o_g], w_in[:, o_q:o_k], gates, w_in[:, o_k:o_ff]], axis=1)
    return w.astype(dtype), None


def _route_weights(rg, rgb, re, reb):
    n = rg.shape[1] + re.shape[1]
    w = jnp.pad(jnp.concatenate([rg, re], axis=1), ((0, 0), (0, ROUTE_W - n)))
    b = jnp.pad(jnp.concatenate([rgb, reb]), (0, ROUTE_W - n)).reshape(1, ROUTE_W)
    return w, b


def _lru_gate_weights(wr, wi, dtype):
    nb, bs, _ = wr.shape
    eye = jnp.eye(nb, dtype=wr.dtype)

    def dense(w):
        return jnp.einsum("ncd,nm->ncmd", w, eye).reshape(nb * bs, nb * bs)

    return jnp.concatenate([dense(wr), dense(wi)], axis=1).astype(dtype)


def _tile(n, preferred):
    return preferred if n % preferred == 0 else n


def kernel(x_prompt, x_sample, cache_k, cache_v, cache_logf, page_table, state_conv, state_lru_h, state_mlstm_C, state_mlstm_n, state_mlstm_m, norm1, w_in, conv_w, conv_b, lru_wr, lru_br, lru_wi, lru_bi, lru_lambda, m_bi, m_bf, fox_bf, out_norm, w_out, norm2, router_g, router_g_b, router_e, router_e_b, w_gate, w_up, w_down, final_norm):
    depth, d = norm1.shape
    lw = conv_w.shape[-1]
    mh = m_bi.shape[-1]
    fh = fox_bf.shape[-1]
    dh = HEAD_DIM
    mw, fw = mh * dh, fh * dh
    dims = (lw, mw, fw)
    n_gate = 2 * mh + fh
    pool, page = cache_k.shape[1], cache_k.shape[2]
    ck = cache_k.transpose(0, 1, 3, 4, 2).reshape(depth, pool, fw, page)
    cv = cache_v.transpose(0, 1, 3, 4, 2).reshape(depth, pool, fw, page)
    clf = cache_logf.transpose(0, 1, 3, 2)

    def layer(x2, b, t, l, conv_buf, h0, c0, n0, m0, prompt):
        n = b * t
        hp = (not prompt) and l < depth - 1
        wdt = F32 if hp else BF16
        w_row, w_t = _perm_w_in(w_in[l], dims, mh, fh, prompt, wdt)
        if prompt:
            row_outs = [(2 * lw, F32, 1.0), (4 * mw, F32, 1.0), (fw, BF16, 1.0), (GATE_W, F32, 1.0)]
            t_outs = [(fw, (BF16,), LOG2E * dh ** -0.5), (fw, (F32,), 1.0), (fw, (F32, BF16), 1.0)]
            lru, mz, kb, gate, qt, kt, vt, vtb = _in_proj(x2, n, norm1[l], w_row, w_t, row_outs, t_outs,
                                                          _tile(n, 512), t, hp)
        else:
            row_outs = [(2 * lw, F32, 1.0), (4 * mw, F32, 1.0), (fw, wdt, dh ** -0.5), (GATE_W, F32, 1.0),
                        (fw, F32, 1.0), (fw, F32, 1.0)]
            lru, mz, fq, gate, k_new, v_new = _in_proj(x2, n, norm1[l], w_row, w_t, row_outs, [], _tile(n, 512),
                                                       t, hp)

        tp = -(-t // CHUNK) * CHUNK
        gate3 = gate.reshape(b, t, GATE_W)
        if tp != t:
            gate3 = jnp.pad(gate3, ((0, 0), (0, tp - t), (0, 0)))
        gbias = jnp.pad(jnp.concatenate([m_bi[l], m_bf[l], fox_bf[l]]), (0, GATE_W - n_gate)).reshape(1, GATE_W)
        act, cum, act_t, cum_t = _gates(gate3, gbias, mh, 2 * mh, 2 * SUBLANES, _tile(tp, 4 * CHUNK))

        cbuf8 = jnp.pad(conv_buf, ((0, 0), (SUBLANES - (CONV_W - 1), 0), (0, 0)))
        wg = _lru_gate_weights(lru_wr[l], lru_wi[l], wdt)
        bg = jnp.concatenate([lru_br[l], lru_bi[l]]).reshape(1, 2 * lw)
        y_l, cn8, hn = _lru(lru.reshape(b, t, 2 * lw), cbuf8, h0.reshape(b, 1, lw), conv_w[l],
                            conv_b[l].reshape(1, lw), wg, bg, lru_lambda[l].reshape(1, lw), _tile(t, 256), hp)

        y_m, c_new, n_new, m_new = _mlstm(mz.reshape(b, t, 4 * mw), act, cum, act_t, cum_t, c0,
                                          n0.reshape(b, mh, 1, dh), m0.reshape(b, mh, 1, 1), mh, hp)

        if prompt:
            tq = _tile(t, 256)
            y_f = _fox_prompt(qt, kb.reshape(b, t, fw), vtb, cum, 2 * mh, tq, tq)
            k_state = kt.reshape(b, fh, dh, t).transpose(0, 3, 1, 2)
            v_state = vt.reshape(b, fh, dh, t).transpose(0, 3, 1, 2)
        else:
            g_t = cum_t[:, 2 * mh:2 * mh + fh, :t]
            y_f = _fox_decode(page_table, fq.reshape(b, t, fw), k_new.reshape(b, t, fw), v_new.reshape(b, t, fw),
                              g_t, ck, cv, clf, l, 16, hp)
            k_state = k_new.reshape(b, t, fh, dh)
            v_state = v_new.reshape(b, t, fh, dh)

        w_route, b_route = _route_weights(router_g[l], router_g_b[l], router_e[l], router_e_b[l])
        xr = _out_proj(x2, n, y_l.reshape(n, lw), y_m.reshape(n, mw), y_f.reshape(n, fw),
                       out_norm[l].reshape(1, d), w_out[l].astype(wdt), norm2[l].reshape(1, d),
                       w_route, b_route, _tile(n, 512), hp)
        x_next = _moe(xr, n, norm2[l].reshape(1, d), w_gate[l].astype(BF16), w_up[l].astype(BF16),
                      w_down[l].astype(BF16), 256 if n >= 4096 else 64)
        lf_state = act_t[:, 2 * mh:2 * mh + fh, :t].transpose(0, 2, 1)
        state = (k_state, v_state, lf_state, cn8[:, SUBLANES - (CONV_W - 1):], hn[:, 0], c_new,
                 n_new.reshape(b, mh, dh), m_new.reshape(b, mh))
        return x_next, state

    bp = x_prompt.shape[0]
    zc = jnp.zeros((bp, CONV_W - 1, lw), F32)
    zh = jnp.zeros((bp, lw), F32)
    z_c = jnp.zeros((bp, mh, dh, dh), F32)
    zn = jnp.zeros((bp, mh, dh), F32)
    zm = jnp.zeros((bp, mh), F32)

    bs, ts = x_sample.shape[0], x_sample.shape[1]
    tp = x_prompt.shape[1]
    xp, xs = x_prompt.reshape(bp * tp, d), x_sample.reshape(bs * ts, d)
    st_p, st_s = [], []
    for l in range(depth):
        xp, sp = layer(xp, bp, tp, l, zc, zh, z_c, zn, zm, True)
        xs, ss = layer(xs, bs, ts, l, state_conv[l], state_lru_h[l], state_mlstm_C[l], state_mlstm_n[l],
                       state_mlstm_m[l], False)
        st_p.append(sp)
        st_s.append(ss)

    def fin(x, b, t):
        return _final_norm(x, b * t, final_norm, _tile(b * t, 512)).reshape(b, t, d)

    def stk(states, i):
        return jnp.stack([s[i] for s in states])

    return ((fin(xp, bp, tp), fin(xs, bs, ts)) + tuple(stk(st_p, i) for i in range(8)) + tuple(stk(st_s, i) for i in range(8)))
```
